```python
import math
import jax, jax.numpy as jnp
from jax import lax
import numpy as np

D_MODEL = 1024
BATCH = 32
SEQ = 2048
DEPTH = 1

N_MEM = 256
HEAD_DIM = 64
N_DIL_HEADS = 8
N_FOX_HEADS = 8
DIL_WIDTH = N_DIL_HEADS * HEAD_DIM
FOX_WIDTH = N_FOX_HEADS * HEAD_DIM
MIX_WIDTH = DIL_WIDTH + FOX_WIDTH
IN_WIDTH = 3 * DIL_WIDTH + 3 * FOX_WIDTH + N_FOX_HEADS
DIL_CONFIGS = ((128, 1), (512, 4), (2048, 16))
BLOCK = 128
N_XATTN_HEADS = 4
XATTN_HEAD_DIM = D_MODEL // N_XATTN_HEADS
D_FF = 4 * D_MODEL
EPS = 1e-6
NEG = -1e30

kernel_name = "hybrid_dilated_fox_block"


def _rmsnorm(x, g):
    x32 = x.astype(jnp.float32)
    y = x32 * lax.rsqrt(jnp.mean(x32 * x32, axis=-1, keepdims=True) + EPS) * g.astype(jnp.float32)
    return y.astype(x.dtype)


def _alibi_slopes(n):
    return 2.0 ** (-(jnp.arange(1, n + 1, dtype=jnp.float32) * (8.0 / n)))


def _dilated_branch(q, k, v, slopes, window, dilation):
    B, S, H, Dh = q.shape
    span = dilation * BLOCK
    s_pad = -(-S // span) * span
    pad = ((0, 0), (0, s_pad - S), (0, 0), (0, 0))
    q, k, v = (jnp.pad(a, pad) for a in (q, k, v))
    L = s_pad // dilation
    nb = L // BLOCK

    def to_res(a):
        return a.reshape(B, L, dilation, H, Dh).transpose(0, 2, 1, 3, 4).reshape(B, dilation, nb, BLOCK, H, Dh)

    def with_prev(a):
        prev = jnp.pad(a, ((0, 0), (0, 0), (1, 0), (0, 0), (0, 0), (0, 0)))[:, :, :-1]
        return jnp.concatenate([prev, a], axis=3)

    qr = to_res(q)
    kb = with_prev(to_res(k))
    vb = with_prev(to_res(v))

    steps = window // dilation
    qi = jnp.arange(BLOCK)[:, None]
    kj = jnp.arange(2 * BLOCK)[None, :]
    delta = qi + BLOCK - kj
    blk = jnp.arange(nb)[:, None, None]
    valid = (delta >= 0) & (delta <= steps) & ((blk > 0) | (kj >= BLOCK))
    dist = (delta * dilation).astype(jnp.float32)
    bias = jnp.where(valid[:, None], -slopes[None, :, None, None] * dist[None, None], NEG)

    scale = 1.0 / math.sqrt(Dh)
    s = jnp.einsum('brnqhd,brnkhd->brnhqk', qr, kb).astype(jnp.float32) * scale + bias[None, None]
    lse = jax.nn.logsumexp(s, axis=-1)
    p = jnp.exp(s - lse[..., None])
    o = jnp.einsum('brnhqk,brnkhd->brnqhd', p, vb.astype(jnp.float32))

    o = o.reshape(B, dilation, L, H, Dh).transpose(0, 2, 1, 3, 4).reshape(B, s_pad, H, Dh)[:, :S]
    lse = lse.transpose(0, 1, 2, 4, 3).reshape(B, dilation, L, H).transpose(0, 2, 1, 3).reshape(B, s_pad, H)[:, :S]
    return o, lse


def _dilated_attention(q, k, v):
    slopes = _alibi_slopes(q.shape[2])
    outs, lses = [], []
    for window, dilation in DIL_CONFIGS:
        o, lse = _dilated_branch(q, k, v, slopes, window, dilation)
        outs.append(o)
        lses.append(lse)
    w = jax.nn.softmax(jnp.stack(lses, axis=0), axis=0)
    o = jnp.sum(w[..., None] * jnp.stack(outs, axis=0), axis=0)
    return o.astype(q.dtype)


def _forgetting_attention(q, k, v, log_f):
    B, S, H, Dh = q.shape
    nb = S // BLOCK
    c = jnp.cumsum(log_f, axis=1).transpose(0, 2, 1)
    qb = q.reshape(B, nb, BLOCK, H, Dh).transpose(1, 0, 2, 3, 4)
    cq = c.reshape(B, H, nb, BLOCK).transpose(2, 0, 1, 3)
    kpos = jnp.arange(S)
    scale = 1.0 / math.sqrt(Dh)

    def one_block(args):
        qblk, cblk, n = args
        s = jnp.einsum('bqhd,bkhd->bhqk', qblk, k).astype(jnp.float32) * scale
        s = s + (cblk[..., :, None] - c[:, :, None, :])
        qpos = n * BLOCK + jnp.arange(BLOCK)
        s = jnp.where((kpos[None, :] <= qpos[:, None])[None, None], s, NEG)
        p = jax.nn.softmax(s, axis=-1)
        return jnp.einsum('bhqk,bkhd->bqhd', p.astype(v.dtype), v)

    o = lax.map(one_block, (qb, cq, jnp.arange(nb)))
    return o.transpose(1, 0, 2, 3, 4).reshape(B, S, H, Dh)


def _hybrid_mixer(h, w_in, b_forget, w_out):
    B, S, _ = h.shape
    z = h @ w_in
    o0 = 3 * DIL_WIDTH
    o1 = o0 + 3 * FOX_WIDTH
    qa, ka, va = jnp.split(z[..., :o0], 3, axis=-1)
    qf, kf, vf = jnp.split(z[..., o0:o1], 3, axis=-1)
    gate = z[..., o1:]
    shp_a = (B, S, N_DIL_HEADS, HEAD_DIM)
    shp_f = (B, S, N_FOX_HEADS, HEAD_DIM)
    ya = _dilated_attention(qa.reshape(shp_a), ka.reshape(shp_a), va.reshape(shp_a))
    log_f = jax.nn.log_sigmoid(gate.astype(jnp.float32) + b_forget.astype(jnp.float32))
    yf = _forgetting_attention(qf.reshape(shp_f), kf.reshape(shp_f), vf.reshape(shp_f), log_f)
    y = jnp.concatenate([ya.reshape(B, S, DIL_WIDTH), yf.reshape(B, S, FOX_WIDTH).astype(ya.dtype)], axis=-1)
    return y @ w_out


def _cross_attention(h, m, w_xq, w_xk, w_xv, w_xo):
    B, S, _ = h.shape
    M = m.shape[1]
    q = (h @ w_xq).reshape(B, S, N_XATTN_HEADS, XATTN_HEAD_DIM)
    k = (m @ w_xk).reshape(B, M, N_XATTN_HEADS, XATTN_HEAD_DIM)
    v = (m @ w_xv).reshape(B, M, N_XATTN_HEADS, XATTN_HEAD_DIM)
    s = jnp.einsum('bqhd,bkhd->bhqk', q, k).astype(jnp.float32) / math.sqrt(XATTN_HEAD_DIM)
    p = jax.nn.softmax(s, axis=-1)
    o = jnp.einsum('bhqk,bkhd->bqhd', p.astype(v.dtype), v).reshape(B, S, N_XATTN_HEADS * XATTN_HEAD_DIM)
    return o @ w_xo


def _sq_relu_mlp(h, w_up, w_down):
    a = jax.nn.relu(h @ w_up)
    return (a * a) @ w_down


def setup_inputs(seed: int = 0) -> dict:
    key = jax.random.key(seed)
    ks = jax.random.split(key, 16)
    f32 = jnp.float32

    def w(k, shape):
        return jax.random.normal(k, shape, f32) * shape[0] ** -0.5

    def gain(k):
        return 1.0 + 0.05 * jax.random.normal(k, (D_MODEL,), f32)

    return {
        "x": jax.random.normal(ks[0], (BATCH, SEQ, D_MODEL), f32),
        "mem": jax.random.normal(ks[1], (BATCH, N_MEM, D_MODEL), f32),
        "g_mix": gain(ks[2]),
        "w_in": w(ks[3], (D_MODEL, IN_WIDTH)),
        "b_forget": 0.1 * jax.random.normal(ks[4], (N_FOX_HEADS,), f32),
        "w_out": w(ks[5], (MIX_WIDTH, D_MODEL)),
        "g_xattn": gain(ks[6]),
        "g_mem": gain(ks[7]),
        "w_xq": w(ks[8], (D_MODEL, N_XATTN_HEADS * XATTN_HEAD_DIM)),
        "w_xk": w(ks[9], (D_MODEL, N_XATTN_HEADS * XATTN_HEAD_DIM)),
        "w_xv": w(ks[10], (D_MODEL, N_XATTN_HEADS * XATTN_HEAD_DIM)),
        "w_xo": w(ks[11], (N_XATTN_HEADS * XATTN_HEAD_DIM, D_MODEL)),
        "g_mlp": gain(ks[12]),
        "w_up": w(ks[13], (D_MODEL, D_FF)),
        "w_down": w(ks[14], (D_FF, D_MODEL)),
        "g_final": gain(ks[15]),
    }


def reference(x, mem, g_mix, w_in, b_forget, w_out, g_xattn, g_mem, w_xq, w_xk, w_xv, w_xo,
              g_mlp, w_up, w_down, g_final):
    m = _rmsnorm(mem, g_mem)
    for _ in range(DEPTH):
        x = x + _hybrid_mixer(_rmsnorm(x, g_mix), w_in, b_forget, w_out)
        x = x + _cross_attention(_rmsnorm(x, g_xattn), m, w_xq, w_xk, w_xv, w_xo)
        x = x + _sq_relu_mlp(_rmsnorm(x, g_mlp), w_up, w_down)
    return _rmsnorm(x, g_final)
```

```python
import functools
import math

import jax
import jax.numpy as jnp
from jax import lax
from jax.experimental import pallas as pl
from jax.experimental.pallas import tpu as pltpu

D_MODEL = 1024
SEQ = 2048
N_MEM = 256
HEAD_DIM = 64
N_DIL_HEADS = 8
N_FOX_HEADS = 8
DIL_WIDTH = N_DIL_HEADS * HEAD_DIM
FOX_WIDTH = N_FOX_HEADS * HEAD_DIM
QKV_WIDTH = 3 * DIL_WIDTH + 3 * FOX_WIDTH
DIL_CONFIGS = ((128, 1), (512, 4), (2048, 16))
BLOCK = 128
N_XATTN_HEADS = 4
XATTN_HEAD_DIM = D_MODEL // N_XATTN_HEADS
D_FF = 4 * D_MODEL
EPS = 1e-6
NEG = -1e30

LANES = 128
HEAD_PAIRS = N_FOX_HEADS // 2
ROW_TILE = 512
FOX_TILE = 256
FF_CHUNK = 1024
MIX_CHUNK = 256

F32 = jnp.float32
BF16 = jnp.bfloat16

_NT = (((1,), (1,)), ((), ()))


def _dot(a, b):
    return jnp.dot(a, b, preferred_element_type=F32)


def _dot_nt(a, b):
    return lax.dot_general(a, b, _NT, preferred_element_type=F32)


def _rmsnorm(x, g):
    return x * lax.rsqrt(jnp.mean(x * x, axis=-1, keepdims=True) + EPS) * g


def _resident(shape):
    zeros = (0,) * len(shape)
    return pl.BlockSpec(shape, lambda *_: zeros, pipeline_mode=pl.Buffered(1))


def _params(semantics, vmem_mib):
    return pltpu.CompilerParams(dimension_semantics=semantics,
                                vmem_limit_bytes=vmem_mib * 1024 * 1024)


def _stack_heads(q2):
    lane = lax.broadcasted_iota(jnp.int32, (1, LANES), 1)
    zero = jnp.zeros_like(q2)
    return jnp.concatenate([jnp.where(lane < HEAD_DIM, q2, zero),
                            jnp.where(lane >= HEAD_DIM, q2, zero)], axis=0)


def _unstack_heads(o):
    rows = o.shape[0] // 2
    lane = lax.broadcasted_iota(jnp.int32, (1, LANES), 1)
    return jnp.where(lane < HEAD_DIM, o[:rows], o[rows:])


def _mem_kv_kernel(mem_ref, g_ref, wk_ref, wv_ref, k_ref, v_ref):
    m = _rmsnorm(mem_ref[0], g_ref[...]).astype(BF16)
    k_ref[0] = _dot(m, wk_ref[...]).astype(BF16)
    v_ref[0] = _dot(m, wv_ref[...]).astype(BF16)


def _mem_kv(mem, g_mem, w_xk, w_xv):
    B = mem.shape[0]
    blk = pl.BlockSpec((1, N_MEM, D_MODEL), lambda b: (b, 0, 0))
    return pl.pallas_call(
        _mem_kv_kernel,
        grid=(B,),
        in_specs=[blk, _resident((1, D_MODEL)), _resident((D_MODEL, D_MODEL)),
                  _resident((D_MODEL, D_MODEL))],
        out_specs=[blk, blk],
        out_shape=[jax.ShapeDtypeStruct((B, N_MEM, D_MODEL), BF16)] * 2,
        compiler_params=_params(("parallel",), 24),
        name="mem_kv",
    )(mem, g_mem, w_xk, w_xv)


def _in_proj_kernel(x_ref, g_ref, w_ref, wg_ref, qa, ka, va, qf, kf, vf, gate):
    h = _rmsnorm(x_ref[...], g_ref[...]).astype(BF16)
    scale = 1.0 / math.sqrt(HEAD_DIM)

    def proj(i):
        return _dot(h, w_ref[:, i * DIL_WIDTH:(i + 1) * DIL_WIDTH])

    qa[...] = proj(0) * scale
    ka[...] = proj(1)
    va[...] = proj(2)
    qf[...] = (proj(3) * scale).astype(BF16)
    kf[...] = proj(4).astype(BF16)
    vf[...] = proj(5).astype(BF16)
    gate[...] = _dot_nt(wg_ref[...], h)


def _in_proj(x2d, g_mix, w_qkv, w_gate_t):
    T = x2d.shape[0]
    row = lambda i: (i, 0)
    out_blk = pl.BlockSpec((ROW_TILE, DIL_WIDTH), row)
    return pl.pallas_call(
        _in_proj_kernel,
        grid=(T // ROW_TILE,),
        in_specs=[pl.BlockSpec((ROW_TILE, D_MODEL), row), _resident((1, D_MODEL)),
                  _resident((D_MODEL, QKV_WIDTH)), _resident((N_FOX_HEADS, D_MODEL))],
        out_specs=[out_blk] * 6 + [pl.BlockSpec((N_FOX_HEADS, ROW_TILE), lambda i: (0, i))],
        out_shape=[jax.ShapeDtypeStruct((T, DIL_WIDTH), F32)] * 3
        + [jax.ShapeDtypeStruct((T, FOX_WIDTH), BF16)] * 3
        + [jax.ShapeDtypeStruct((N_FOX_HEADS, T), F32)],
        compiler_params=_params(("parallel",), 40),
        name="in_proj",
    )(x2d, g_mix, w_qkv, w_gate_t)


def _gate_cum_kernel(gate_ref, b_ref, c_ref):
    r = lax.broadcasted_iota(jnp.int32, (BLOCK, BLOCK), 0)
    c = lax.broadcasted_iota(jnp.int32, (BLOCK, BLOCK), 1)
    upper = (r <= c).astype(BF16)
    carry = jnp.zeros((N_FOX_HEADS, 1), F32)
    for blk in range(SEQ // BLOCK):
        cols = slice(blk * BLOCK, (blk + 1) * BLOCK)
        z = gate_ref[:, cols] + b_ref[...]
        log_f = jnp.minimum(z, 0.0) - jnp.log1p(jnp.exp(-jnp.abs(z)))
        hi = log_f.astype(BF16)
        rest = log_f - hi.astype(F32)
        mid = rest.astype(BF16)
        lo = (rest - mid.astype(F32)).astype(BF16)
        cum = _dot(hi, upper) + _dot(mid, upper) + _dot(lo, upper) + carry
        c_ref[0, :, cols] = cum
        carry = cum[:, BLOCK - 1:BLOCK]


def _gate_cum(gate_t, b_forget, B):
    return pl.pallas_call(
        _gate_cum_kernel,
        grid=(B,),
        in_specs=[pl.BlockSpec((N_FOX_HEADS, SEQ), lambda b: (0, b)),
                  _resident((N_FOX_HEADS, 1))],
        out_specs=pl.BlockSpec((1, N_FOX_HEADS, SEQ), lambda b: (b, 0, 0)),
        out_shape=jax.ShapeDtypeStruct((B, N_FOX_HEADS, SEQ), F32),
        compiler_params=_params(("parallel",), 16),
        name="gate_cum",
    )(gate_t, b_forget)


def _dilated_kernel(slopes_ref, q_ref, k_ref, v_ref, o_ref, ob_ref, lb_ref):
    pair = pl.program_id(1)
    row = lax.broadcasted_iota(jnp.int32, (2 * BLOCK, 1), 0)
    slope = jnp.where(row < BLOCK, slopes_ref[2 * pair], slopes_ref[2 * pair + 1])
    qi = jnp.where(row < BLOCK, row, row - BLOCK)
    lane = lax.broadcasted_iota(jnp.int32, (1, LANES), 1)

    def rows(start, size, dilation):
        if dilation == 1:
            return pl.ds(start, size)
        return pl.ds(start, size, stride=dilation)

    for branch, (window, dilation) in enumerate(DIL_CONFIGS):
        steps = window // dilation
        length = SEQ // dilation
        n_blocks = length // BLOCK

        def bias_table(n_keys, first_key):
            kj = lax.broadcasted_iota(jnp.int32, (1, n_keys), 1) + first_key
            delta = qi - kj
            valid = (delta >= 0) & (delta <= steps)
            dist = (delta * dilation).astype(F32)
            return jnp.where(valid, -slope * dist, NEG)

        bias_first = bias_table(BLOCK, 0)
        bias_rest = bias_table(2 * BLOCK, -BLOCK)

        def attend(q_start, k_start, n_keys, bias):
            q2 = q_ref[0, rows(q_start, BLOCK, dilation), :].astype(BF16)
            k2 = k_ref[0, rows(k_start, n_keys, dilation), :].astype(BF16)
            v2 = v_ref[0, rows(k_start, n_keys, dilation), :].astype(BF16)
            s = _dot_nt(_stack_heads(q2), k2) + bias
            m = jnp.max(s, axis=-1, keepdims=True)
            p = jnp.exp(s - m)
            l = jnp.sum(p, axis=-1, keepdims=True)
            o = _dot(p.astype(BF16), v2) / l
            lse = jnp.broadcast_to(m + jnp.log(l), (2 * BLOCK, LANES))
            ob_ref[branch, rows(q_start, BLOCK, dilation), :] = _unstack_heads(o)
            lb_ref[branch, rows(q_start, BLOCK, dilation), :] = _unstack_heads(lse)

        def per_residue(r, carry):
            attend(r, r, BLOCK, bias_first)

            def per_block(n, carry):
                attend(r + n * (dilation * BLOCK), r + (n - 1) * (dilation * BLOCK),
                       2 * BLOCK, bias_rest)
                return carry

            return lax.fori_loop(1, n_blocks, per_block, carry)

        if dilation == 1:
            per_residue(0, 0)
        else:
            lax.fori_loop(0, dilation, per_residue, 0)

    def mix(i, carry):
        chunk = pl.ds(pl.multiple_of(i * MIX_CHUNK, MIX_CHUNK), MIX_CHUNK)
        lses = [lb_ref[b, chunk, :] for b in range(len(DIL_CONFIGS))]
        top = functools.reduce(jnp.maximum, lses)
        ws = [jnp.exp(l - top) for l in lses]
        num = sum(w * ob_ref[b, chunk, :] for b, w in enumerate(ws))
        o_ref[0, chunk, :] = (num / sum(ws)).astype(o_ref.dtype)
        return carry

    lax.fori_loop(0, SEQ // MIX_CHUNK, mix, 0)


def _dilated(slopes, qa, ka, va):
    B = qa.shape[0]
    blk = pl.BlockSpec((1, SEQ, LANES), lambda b, p: (b, 0, p))
    n_br = len(DIL_CONFIGS)
    return pl.pallas_call(
        _dilated_kernel,
        grid=(B, HEAD_PAIRS),
        in_specs=[pl.BlockSpec(memory_space=pltpu.SMEM), blk, blk, blk],
        out_specs=blk,
        out_shape=jax.ShapeDtypeStruct((B, SEQ, DIL_WIDTH), BF16),
        scratch_shapes=[pltpu.VMEM((n_br, SEQ, LANES), F32),
                        pltpu.VMEM((n_br, SEQ, LANES), F32)],
        compiler_params=_params(("parallel", "parallel"), 32),
        name="dilated",
    )(slopes, qa, ka, va)


def _fox_kernel(q_ref, k_ref, v_ref, c_ref, o_ref):
    tile = FOX_TILE
    qpos = lax.broadcasted_iota(jnp.int32, (tile, tile), 0)
    kpos = lax.broadcasted_iota(jnp.int32, (tile, tile), 1)
    causal = (kpos <= qpos)[None]

    def per_query_tile(i, carry):
        q_rows = pl.ds(pl.multiple_of(i * tile, tile), tile)
        qs = _stack_heads(q_ref[0, q_rows, :])
        c_q = c_ref[0, 0, :, q_rows]
        c_base = jnp.max(c_q, axis=-1, keepdims=True)

        def scores(j):
            k_rows = pl.ds(pl.multiple_of(j * tile, tile), tile)
            s = _dot_nt(qs, k_ref[0, k_rows, :]).reshape(2, tile, tile)
            c_k = c_ref[0, 0, :, k_rows] - c_base
            return s - c_k[:, None, :], k_rows

        def update(s, k_rows, state):
            m, l, acc = state
            m_new = jnp.maximum(m, jnp.max(s, axis=-1, keepdims=True))
            alpha = jnp.exp(m - m_new)
            p = jnp.exp(s - m_new)
            l = alpha * l + jnp.sum(p, axis=-1, keepdims=True)
            pv = _dot(p.astype(BF16).reshape(2 * tile, tile), v_ref[0, k_rows, :])
            acc = alpha * acc + pv.reshape(2, tile, LANES)
            return m_new, l, acc

        def full_tile(j, state):
            s, k_rows = scores(j)
            return update(s, k_rows, state)

        state = (jnp.full((2, tile, 1), NEG, F32), jnp.zeros((2, tile, 1), F32),
                 jnp.zeros((2, tile, LANES), F32))
        state = lax.fori_loop(0, i, full_tile, state)
        s, k_rows = scores(i)
        _, l, acc = update(jnp.where(causal, s, NEG), k_rows, state)
        o = (acc / l).reshape(2 * tile, LANES)
        o_ref[0, q_rows, :] = _unstack_heads(o).astype(o_ref.dtype)
        return carry

    lax.fori_loop(0, SEQ // tile, per_query_tile, 0)


def _fox(qf, kf, vf, c):
    B = qf.shape[0]
    blk = pl.BlockSpec((1, SEQ, LANES), lambda b, p: (b, 0, p))
    return pl.pallas_call(
        _fox_kernel,
        grid=(B, HEAD_PAIRS),
        in_specs=[blk, blk, blk,
                  pl.BlockSpec((1, 1, 2, SEQ), lambda b, p: (b, p, 0, 0))],
        out_specs=blk,
        out_shape=jax.ShapeDtypeStruct((B, SEQ, FOX_WIDTH), BF16),
        compiler_params=_params(("parallel", "parallel"), 32),
        name="fox",
    )(qf, kf, vf, c)


def _post_mix_kernel(x_ref, ya_ref, yf_ref, km_ref, vm_ref, woa_ref, wob_ref, g_ref,
                     wq_ref, wo_ref, o_ref):
    x = x_ref[0] + _dot(ya_ref[0], woa_ref[...]) + _dot(yf_ref[0], wob_ref[...])
    h = _rmsnorm(x, g_ref[...]).astype(BF16)
    scale = 1.0 / math.sqrt(XATTN_HEAD_DIM)
    q = (_dot(h, wq_ref[...]) * scale).astype(BF16)
    heads = []
    for hd in range(N_XATTN_HEADS):
        cols = slice(hd * XATTN_HEAD_DIM, (hd + 1) * XATTN_HEAD_DIM)
        s = _dot_nt(q[:, cols], km_ref[0, :, cols])
        p = jnp.exp(s - jnp.max(s, axis=-1, keepdims=True))
        l = jnp.sum(p, axis=-1, keepdims=True)
        heads.append((_dot(p.astype(BF16), vm_ref[0, :, cols]) / l).astype(BF16))
    o_ref[0] = x + _dot(jnp.concatenate(heads, axis=-1), wo_ref[...])


def _post_mix(x, ya, yf, km, vm, w_oa, w_ob, g_xattn, w_xq, w_xo):
    B = x.shape[0]
    tok = lambda b, i: (b, i, 0)
    per_b = lambda b, i: (b, 0, 0)
    return pl.pallas_call(
        _post_mix_kernel,
        grid=(B, SEQ // ROW_TILE),
        in_specs=[pl.BlockSpec((1, ROW_TILE, D_MODEL), tok),
                  pl.BlockSpec((1, ROW_TILE, DIL_WIDTH), tok),
                  pl.BlockSpec((1, ROW_TILE, FOX_WIDTH), tok),
                  pl.BlockSpec((1, N_MEM, D_MODEL), per_b),
                  pl.BlockSpec((1, N_MEM, D_MODEL), per_b),
                  _resident((DIL_WIDTH, D_MODEL)), _resident((FOX_WIDTH, D_MODEL)),
                  _resident((1, D_MODEL)), _resident((D_MODEL, D_MODEL)),
                  _resident((D_MODEL, D_MODEL))],
        out_specs=pl.BlockSpec((1, ROW_TILE, D_MODEL), tok),
        out_shape=jax.ShapeDtypeStruct(x.shape, F32),
        compiler_params=_params(("parallel", "parallel"), 40),
        name="post_mix",
    )(x, ya, yf, km, vm, w_oa, w_ob, g_xattn, w_xq, w_xo)


def _mlp_kernel(x_ref, g_ref, wu_ref, wd_ref, gf_ref, o_ref):
    x = x_ref[...]
    h = _rmsnorm(x, g_ref[...]).astype(BF16)
    y = x
    for c in range(D_FF // FF_CHUNK):
        cols = slice(c * FF_CHUNK, (c + 1) * FF_CHUNK)
        a = jnp.maximum(_dot(h, wu_ref[:, cols]), 0.0)
        y = y + _dot((a * a).astype(BF16), wd_ref[cols, :])
    o_ref[...] = _rmsnorm(y, gf_ref[...])


def _mlp(x2d, g_mlp, w_up, w_down, g_final):
    T = x2d.shape[0]
    row = pl.BlockSpec((ROW_TILE, D_MODEL), lambda i: (i, 0))
    return pl.pallas_call(
        _mlp_kernel,
        grid=(T // ROW_TILE,),
        in_specs=[row, _resident((1, D_MODEL)), _resident((D_MODEL, D_FF)),
                  _resident((D_FF, D_MODEL)), _resident((1, D_MODEL))],
        out_specs=row,
        out_shape=jax.ShapeDtypeStruct(x2d.shape, F32),
        compiler_params=_params(("parallel",), 48),
        name="mlp",
    )(x2d, g_mlp, w_up, w_down, g_final)


def kernel(x, mem, g_mix, w_in, b_forget, w_out, g_xattn, g_mem, w_xq, w_xk, w_xv, w_xo,
           g_mlp, w_up, w_down, g_final):
    B, S, D = x.shape
    assert (S, D) == (SEQ, D_MODEL) and mem.shape == (B, N_MEM, D_MODEL)
    gain = lambda g: g.reshape(1, D_MODEL).astype(F32)
    bf = lambda w: w.astype(BF16)
    slopes = 2.0 ** (-(jnp.arange(1, N_DIL_HEADS + 1, dtype=F32) * (8.0 / N_DIL_HEADS)))

    km, vm = _mem_kv(mem, gain(g_mem), bf(w_xk), bf(w_xv))
    qa, ka, va, qf, kf, vf, gate_t = _in_proj(
        x.reshape(B * S, D), gain(g_mix), bf(w_in[:, :QKV_WIDTH]), bf(w_in[:, QKV_WIDTH:].T))
    c = _gate_cum(gate_t, b_forget.reshape(N_FOX_HEADS, 1).astype(F32), B)
    seq = lambda a: a.reshape(B, S, a.shape[-1])
    ya = _dilated(slopes, seq(qa), seq(ka), seq(va))
    yf = _fox(seq(qf), seq(kf), seq(vf), c.reshape(B, HEAD_PAIRS, 2, S))
    x = _post_mix(x, ya, yf, km, vm, bf(w_out[:DIL_WIDTH]), bf(w_out[DIL_WIDTH:]),
                  gain(g_xattn), bf(w_xq), bf(w_xo))
    y = _mlp(x.reshape(B * S, D), gain(g_mlp), bf(w_up), bf(w_down), gain(g_final))
    return y.reshape(B, S, D)
```

```python
import functools
import math

import jax
import jax.numpy as jnp
from jax import lax
from jax.experimental import pallas as pl
from jax.experimental.pallas import tpu as pltpu

D_MODEL = 1024
SEQ = 2048
N_MEM = 256
HEAD_DIM = 64
N_DIL_HEADS = 8
N_FOX_HEADS = 8
DIL_WIDTH = N_DIL_HEADS * HEAD_DIM
FOX_WIDTH = N_FOX_HEADS * HEAD_DIM
QKV_WIDTH = 3 * DIL_WIDTH + 3 * FOX_WIDTH
DIL_CONFIGS = ((128, 1), (512, 4), (2048, 16))
BLOCK = 128
N_XATTN_HEADS = 4
XATTN_HEAD_DIM = D_MODEL // N_XATTN_HEADS
D_FF = 4 * D_MODEL
EPS = 1e-6
NEG = -1e30

LANES = 128
HEAD_PAIRS = N_FOX_HEADS // 2
ROW_TILE = 512
FOX_TILE = 256
FF_CHUNK = 1024
MIX_CHUNK = 256
DIL_GROUP = 4

F32 = jnp.float32
BF16 = jnp.bfloat16

_NT = (((1,), (1,)), ((), ()))


def _dot(a, b):
    return jnp.dot(a, b, preferred_element_type=F32)


def _dot_nt(a, b):
    return lax.dot_general(a, b, _NT, preferred_element_type=F32)


def _rmsnorm(x, g):
    return x * lax.rsqrt(jnp.mean(x * x, axis=-1, keepdims=True) + EPS) * g


def _resident(shape):
    zeros = (0,) * len(shape)
    return pl.BlockSpec(shape, lambda *_: zeros, pipeline_mode=pl.Buffered(1))


def _params(semantics, vmem_mib):
    return pltpu.CompilerParams(dimension_semantics=semantics,
                                vmem_limit_bytes=vmem_mib * 1024 * 1024)


def _stack_heads(q2):
    lane = lax.broadcasted_iota(jnp.int32, (1, LANES), 1)
    zero = jnp.zeros_like(q2)
    return jnp.concatenate([jnp.where(lane < HEAD_DIM, q2, zero),
                            jnp.where(lane >= HEAD_DIM, q2, zero)], axis=0)


def _unstack_heads(o):
    rows = o.shape[0] // 2
    lane = lax.broadcasted_iota(jnp.int32, (1, LANES), 1)
    return jnp.where(lane < HEAD_DIM, o[:rows], o[rows:])


def _mem_kv_kernel(mem_ref, g_ref, wk_ref, wv_ref, k_ref, v_ref):
    m = _rmsnorm(mem_ref[0], g_ref[...]).astype(BF16)
    k_ref[0] = _dot(m, wk_ref[...]).astype(BF16)
    v_ref[0] = _dot(m, wv_ref[...]).astype(BF16)


def _mem_kv(mem, g_mem, w_xk, w_xv):
    B = mem.shape[0]
    blk = pl.BlockSpec((1, N_MEM, D_MODEL), lambda b: (b, 0, 0))
    return pl.pallas_call(
        _mem_kv_kernel,
        grid=(B,),
        in_specs=[blk, _resident((1, D_MODEL)), _resident((D_MODEL, D_MODEL)),
                  _resident((D_MODEL, D_MODEL))],
        out_specs=[blk, blk],
        out_shape=[jax.ShapeDtypeStruct((B, N_MEM, D_MODEL), BF16)] * 2,
        compiler_params=_params(("parallel",), 24),
        name="mem_kv",
    )(mem, g_mem, w_xk, w_xv)


def _in_proj_kernel(x_ref, g_ref, w_ref, wg_ref, qa, ka, va, qf, kf, vf, gate):
    h = _rmsnorm(x_ref[...], g_ref[...]).astype(BF16)
    scale = 1.0 / math.sqrt(HEAD_DIM)

    def proj(i):
        return _dot(h, w_ref[:, i * DIL_WIDTH:(i + 1) * DIL_WIDTH])

    qa[...] = proj(0) * scale
    ka[...] = proj(1)
    va[...] = proj(2)
    qf[...] = (proj(3) * scale).astype(BF16)
    kf[...] = proj(4).astype(BF16)
    vf[...] = proj(5).astype(BF16)
    gate[...] = _dot_nt(wg_ref[...], h)


def _in_proj(x2d, g_mix, w_qkv, w_gate_t):
    T = x2d.shape[0]
    row = lambda i: (i, 0)
    out_blk = pl.BlockSpec((ROW_TILE, DIL_WIDTH), row)
    return pl.pallas_call(
        _in_proj_kernel,
        grid=(T // ROW_TILE,),
        in_specs=[pl.BlockSpec((ROW_TILE, D_MODEL), row), _resident((1, D_MODEL)),
                  _resident((D_MODEL, QKV_WIDTH)), _resident((N_FOX_HEADS, D_MODEL))],
        out_specs=[out_blk] * 6 + [pl.BlockSpec((N_FOX_HEADS, ROW_TILE), lambda i: (0, i))],
        out_shape=[jax.ShapeDtypeStruct((T, DIL_WIDTH), F32)] * 3
        + [jax.ShapeDtypeStruct((T, FOX_WIDTH), BF16)] * 3
        + [jax.ShapeDtypeStruct((N_FOX_HEADS, T), F32)],
        compiler_params=_params(("parallel",), 40),
        name="in_proj",
    )(x2d, g_mix, w_qkv, w_gate_t)


def _gate_cum_kernel(gate_ref, b_ref, c_ref):
    r = lax.broadcasted_iota(jnp.int32, (BLOCK, BLOCK), 0)
    c = lax.broadcasted_iota(jnp.int32, (BLOCK, BLOCK), 1)
    upper = (r <= c).astype(BF16)
    carry = jnp.zeros((N_FOX_HEADS, 1), F32)
    for blk in range(SEQ // BLOCK):
        cols = slice(blk * BLOCK, (blk + 1) * BLOCK)
        z = gate_ref[:, cols] + b_ref[...]
        log_f = jnp.minimum(z, 0.0) - jnp.log1p(jnp.exp(-jnp.abs(z)))
        hi = log_f.astype(BF16)
        rest = log_f - hi.astype(F32)
        mid = rest.astype(BF16)
        lo = (rest - mid.astype(F32)).astype(BF16)
        cum = _dot(hi, upper) + _dot(mid, upper) + _dot(lo, upper) + carry
        c_ref[0, :, cols] = cum
        carry = cum[:, BLOCK - 1:BLOCK]


def _gate_cum(gate_t, b_forget, B):
    return pl.pallas_call(
        _gate_cum_kernel,
        grid=(B,),
        in_specs=[pl.BlockSpec((N_FOX_HEADS, SEQ), lambda b: (0, b)),
                  _resident((N_FOX_HEADS, 1))],
        out_specs=pl.BlockSpec((1, N_FOX_HEADS, SEQ), lambda b: (b, 0, 0)),
        out_shape=jax.ShapeDtypeStruct((B, N_FOX_HEADS, SEQ), F32),
        compiler_params=_params(("parallel",), 16),
        name="gate_cum",
    )(gate_t, b_forget)


def _dilated_kernel(slopes_ref, q_ref, k_ref, v_ref, o_ref, ob_ref, lb_ref):
    pair = pl.program_id(1)
    row = lax.broadcasted_iota(jnp.int32, (2 * BLOCK, 1), 0)
    slope = jnp.where(row < BLOCK, slopes_ref[2 * pair], slopes_ref[2 * pair + 1])
    qi = jnp.where(row < BLOCK, row, row - BLOCK)
    lane = lax.broadcasted_iota(jnp.int32, (1, LANES), 1)

    def rows(start, size, dilation):
        if dilation == 1:
            return pl.ds(start, size)
        return pl.ds(start, size, stride=dilation)

    for branch, (window, dilation) in enumerate(DIL_CONFIGS):
        steps = window // dilation
        length = SEQ // dilation
        n_blocks = length // BLOCK

        def bias_table(n_keys, first_key):
            kj = lax.broadcasted_iota(jnp.int32, (1, n_keys), 1) + first_key
            delta = qi - kj
            valid = (delta >= 0) & (delta <= steps)
            dist = (delta * dilation).astype(F32)
            return jnp.where(valid, -slope * dist, NEG)

        bias_first = bias_table(BLOCK, 0)
        bias_rest = bias_table(2 * BLOCK, -BLOCK)

        def attend(q_start, k_start, n_keys, bias):
            q2 = q_ref[0, rows(q_start, BLOCK, dilation), :].astype(BF16)
            k2 = k_ref[0, rows(k_start, n_keys, dilation), :].astype(BF16)
            v2 = v_ref[0, rows(k_start, n_keys, dilation), :].astype(BF16)
            s = _dot_nt(_stack_heads(q2), k2) + bias
            m = jnp.max(s, axis=-1, keepdims=True)
            p = jnp.exp(s - m)
            l = jnp.sum(p, axis=-1, keepdims=True)
            o = _dot(p.astype(BF16), v2) / l
            lse = jnp.broadcast_to(m + jnp.log(l), (2 * BLOCK, LANES))
            ob_ref[branch, rows(q_start, BLOCK, dilation), :] = _unstack_heads(o)
            lb_ref[branch, rows(q_start, BLOCK, dilation), :] = _unstack_heads(lse)

        def first_block(r):
            attend(r, r, BLOCK, bias_first)

        def later_block(r, n):
            attend(r + n * (dilation * BLOCK), r + (n - 1) * (dilation * BLOCK),
                   2 * BLOCK, bias_rest)

        if n_blocks == 1:
            def residue_group(g, carry):
                for u in range(DIL_GROUP):
                    first_block(g * DIL_GROUP + u)
                return carry

            lax.fori_loop(0, dilation // DIL_GROUP, residue_group, 0)
        elif n_blocks <= DIL_GROUP:
            def whole_residue(r, carry):
                first_block(r)
                for n in range(1, n_blocks):
                    later_block(r, n)
                return carry

            lax.fori_loop(0, dilation, whole_residue, 0)
        else:
            group = max(g for g in range(1, DIL_GROUP + 1) if (n_blocks - 1) % g == 0)
            for r in range(dilation):
                first_block(r)

                def block_group(i, carry, r=r):
                    for u in range(group):
                        later_block(r, 1 + i * group + u)
                    return carry

                lax.fori_loop(0, (n_blocks - 1) // group, block_group, 0)

    def mix(i, carry):
        chunk = pl.ds(pl.multiple_of(i * MIX_CHUNK, MIX_CHUNK), MIX_CHUNK)
        lses = [lb_ref[b, chunk, :] for b in range(len(DIL_CONFIGS))]
        top = functools.reduce(jnp.maximum, lses)
        ws = [jnp.exp(l - top) for l in lses]
        num = sum(w * ob_ref[b, chunk, :] for b, w in enumerate(ws))
        o_ref[0, chunk, :] = (num / sum(ws)).astype(o_ref.dtype)
        return carry

    lax.fori_loop(0, SEQ // MIX_CHUNK, mix, 0)


def _dilated(slopes, qa, ka, va):
    B = qa.shape[0]
    blk = pl.BlockSpec((1, SEQ, LANES), lambda b, p: (b, 0, p))
    n_br = len(DIL_CONFIGS)
    return pl.pallas_call(
        _dilated_kernel,
        grid=(B, HEAD_PAIRS),
        in_specs=[pl.BlockSpec(memory_space=pltpu.SMEM), blk, blk, blk],
        out_specs=blk,
        out_shape=jax.ShapeDtypeStruct((B, SEQ, DIL_WIDTH), BF16),
        scratch_shapes=[pltpu.VMEM((n_br, SEQ, LANES), F32),
                        pltpu.VMEM((n_br, SEQ, LANES), F32)],
        compiler_params=_params(("parallel", "parallel"), 32),
        name="dilated",
    )(slopes, qa, ka, va)


def _fox_kernel(q_ref, k_ref, v_ref, c_ref, o_ref):
    tile = FOX_TILE
    qpos = lax.broadcasted_iota(jnp.int32, (tile, tile), 0)
    kpos = lax.broadcasted_iota(jnp.int32, (tile, tile), 1)
    causal = (kpos <= qpos)[None]

    def per_query_tile(i, carry):
        q_rows = pl.ds(pl.multiple_of(i * tile, tile), tile)
        qs = _stack_heads(q_ref[0, q_rows, :])
        c_q = c_ref[0, 0, :, q_rows]
        c_base = jnp.max(c_q, axis=-1, keepdims=True)

        def key_rows(j):
            return pl.ds(pl.multiple_of(j * tile, tile), tile)

        def scores(j):
            s = _dot_nt(qs, k_ref[0, key_rows(j), :]).reshape(2, tile, tile)
            c_k = c_ref[0, 0, :, key_rows(j)] - c_base
            return s - c_k[:, None, :]

        def update(s, j, state):
            m, l, acc = state
            m_new = jnp.maximum(m, jnp.max(s, axis=-1, keepdims=True))
            alpha = jnp.exp(m - m_new)
            p = jnp.exp(s - m_new)
            l = alpha * l + jnp.sum(p, axis=-1, keepdims=True)
            pv = _dot(p.astype(BF16).reshape(2 * tile, tile), v_ref[0, key_rows(j), :])
            acc = alpha * acc + pv.reshape(2, tile, LANES)
            return m_new, l, acc

        def full_tile(j, carried):
            s, state = carried
            s_next = scores(j + 1)
            return s_next, update(s, j, state)

        state = (jnp.full((2, tile, 1), NEG, F32), jnp.zeros((2, tile, 1), F32),
                 jnp.zeros((2, tile, LANES), F32))
        s, state = lax.fori_loop(0, i, full_tile, (scores(0), state))
        _, l, acc = update(jnp.where(causal, s, NEG), i, state)
        o = (acc / l).reshape(2 * tile, LANES)
        o_ref[0, q_rows, :] = _unstack_heads(o).astype(o_ref.dtype)
        return carry

    lax.fori_loop(0, SEQ // tile, per_query_tile, 0)


def _fox(qf, kf, vf, c):
    B = qf.shape[0]
    blk = pl.BlockSpec((1, SEQ, LANES), lambda b, p: (b, 0, p))
    return pl.pallas_call(
        _fox_kernel,
        grid=(B, HEAD_PAIRS),
        in_specs=[blk, blk, blk,
                  pl.BlockSpec((1, 1, 2, SEQ), lambda b, p: (b, p, 0, 0))],
        out_specs=blk,
        out_shape=jax.ShapeDtypeStruct((B, SEQ, FOX_WIDTH), BF16),
        compiler_params=_params(("parallel", "parallel"), 32),
        name="fox",
    )(qf, kf, vf, c)


def _post_mix_kernel(x_ref, ya_ref, yf_ref, km_ref, vm_ref, woa_ref, wob_ref, g_ref,
                     wq_ref, wo_ref, o_ref):
    x = x_ref[0] + _dot(ya_ref[0], woa_ref[...]) + _dot(yf_ref[0], wob_ref[...])
    h = _rmsnorm(x, g_ref[...]).astype(BF16)
    scale = 1.0 / math.sqrt(XATTN_HEAD_DIM)
    q = (_dot(h, wq_ref[...]) * scale).astype(BF16)
    heads = []
    for hd in range(N_XATTN_HEADS):
        cols = slice(hd * XATTN_HEAD_DIM, (hd + 1) * XATTN_HEAD_DIM)
        s = _dot_nt(q[:, cols], km_ref[0, :, cols])
        p = jnp.exp(s - jnp.max(s, axis=-1, keepdims=True))
        l = jnp.sum(p, axis=-1, keepdims=True)
        heads.append((_dot(p.astype(BF16), vm_ref[0, :, cols]) / l).astype(BF16))
    o_ref[0] = x + _dot(jnp.concatenate(heads, axis=-1), wo_ref[...])


def _post_mix(x, ya, yf, km, vm, w_oa, w_ob, g_xattn, w_xq, w_xo):
    B = x.shape[0]
    tok = lambda b, i: (b, i, 0)
    per_b = lambda b, i: (b, 0, 0)
    return pl.pallas_call(
        _post_mix_kernel,
        grid=(B, SEQ // ROW_TILE),
        in_specs=[pl.BlockSpec((1, ROW_TILE, D_MODEL), tok),
                  pl.BlockSpec((1, ROW_TILE, DIL_WIDTH), tok),
                  pl.BlockSpec((1, ROW_TILE, FOX_WIDTH), tok),
                  pl.BlockSpec((1, N_MEM, D_MODEL), per_b),
                  pl.BlockSpec((1, N_MEM, D_MODEL), per_b),
                  _resident((DIL_WIDTH, D_MODEL)), _resident((FOX_WIDTH, D_MODEL)),
                  _resident((1, D_MODEL)), _resident((D_MODEL, D_MODEL)),
                  _resident((D_MODEL, D_MODEL))],
        out_specs=pl.BlockSpec((1, ROW_TILE, D_MODEL), tok),
        out_shape=jax.ShapeDtypeStruct(x.shape, F32),
        compiler_params=_params(("parallel", "parallel"), 40),
        name="post_mix",
    )(x, ya, yf, km, vm, w_oa, w_ob, g_xattn, w_xq, w_xo)


def _mlp_kernel(x_ref, g_ref, wu_ref, wd_ref, gf_ref, o_ref):
    x = x_ref[...]
    h = _rmsnorm(x, g_ref[...]).astype(BF16)
    y = x
    for c in range(D_FF // FF_CHUNK):
        cols = slice(c * FF_CHUNK, (c + 1) * FF_CHUNK)
        a = jnp.maximum(_dot(h, wu_ref[:, cols]), 0.0)
        y = y + _dot((a * a).astype(BF16), wd_ref[cols, :])
    o_ref[...] = _rmsnorm(y, gf_ref[...])


def _mlp(x2d, g_mlp, w_up, w_down, g_final):
    T = x2d.shape[0]
    row = pl.BlockSpec((ROW_TILE, D_MODEL), lambda i: (i, 0))
    return pl.pallas_call(
        _mlp_kernel,
        grid=(T // ROW_TILE,),
        in_specs=[row, _resident((1, D_MODEL)), _resident((D_MODEL, D_FF)),
                  _resident((D_FF, D_MODEL)), _resident((1, D_MODEL))],
        out_specs=row,
        out_shape=jax.ShapeDtypeStruct(x2d.shape, F32),
        compiler_params=_params(("parallel",), 48),
        name="mlp",
    )(x2d, g_mlp, w_up, w_down, g_final)


def kernel(x, mem, g_mix, w_in, b_forget, w_out, g_xattn, g_mem, w_xq, w_xk, w_xv, w_xo,
           g_mlp, w_up, w_down, g_final):
    B, S, D = x.shape
    assert (S, D) == (SEQ, D_MODEL) and mem.shape == (B, N_MEM, D_MODEL)
    gain = lambda g: g.reshape(1, D_MODEL).astype(F32)
    bf = lambda w: w.astype(BF16)
    slopes = 2.0 ** (-(jnp.arange(1, N_DIL_HEADS + 1, dtype=F32) * (8.0 / N_DIL_HEADS)))

    km, vm = _mem_kv(mem, gain(g_mem), bf(w_xk), bf(w_xv))
    qa, ka, va, qf, kf, vf, gate_t = _in_proj(
        x.reshape(B * S, D), gain(g_mix), bf(w_in[:, :QKV_WIDTH]), bf(w_in[:, QKV_WIDTH:].T))
    c = _gate_cum(gate_t, b_forget.reshape(N_FOX_HEADS, 1).astype(F32), B)
    seq = lambda a: a.reshape(B, S, a.shape[-1])
    ya = _dilated(slopes, seq(qa), seq(ka), seq(va))
    yf = _fox(seq(qf), seq(kf), seq(vf), c.reshape(B, HEAD_PAIRS, 2, S))
    x = _post_mix(x, ya, yf, km, vm, bf(w_out[:DIL_WIDTH]), bf(w_out[DIL_WIDTH:]),
                  gain(g_xattn), bf(w_xq), bf(w_xo))
    y = _mlp(x.reshape(B * S, D), gain(g_mlp), bf(w_up), bf(w_down), gain(g_final))
    return y.reshape(B, S, D)
```

```python
import functools
import math

import jax
import jax.numpy as jnp
from jax import lax
from jax.experimental import pallas as pl
from jax.experimental.pallas import tpu as pltpu

D_MODEL = 1024
SEQ = 2048
N_MEM = 256
HEAD_DIM = 64
N_DIL_HEADS = 8
N_FOX_HEADS = 8
DIL_WIDTH = N_DIL_HEADS * HEAD_DIM
FOX_WIDTH = N_FOX_HEADS * HEAD_DIM
QKV_WIDTH = 3 * DIL_WIDTH + 3 * FOX_WIDTH
DIL_CONFIGS = ((128, 1), (512, 4), (2048, 16))
BLOCK = 128
N_XATTN_HEADS = 4
XATTN_HEAD_DIM = D_MODEL // N_XATTN_HEADS
D_FF = 4 * D_MODEL
EPS = 1e-6
NEG = -1e30

LANES = 128
HEAD_PAIRS = N_FOX_HEADS // 2
ROW_TILE = 512
FOX_TILE = 256
FOX_HEADS_PER_STEP = 4
FF_CHUNK = 1024
MIX_CHUNK = 256
DIL_GROUP = 4

F32 = jnp.float32
BF16 = jnp.bfloat16

_NT = (((1,), (1,)), ((), ()))


def _dot(a, b):
    return jnp.dot(a, b, preferred_element_type=F32)


def _dot_nt(a, b):
    return lax.dot_general(a, b, _NT, preferred_element_type=F32)


def _rmsnorm(x, g):
    return x * lax.rsqrt(jnp.mean(x * x, axis=-1, keepdims=True) + EPS) * g


def _resident(shape):
    zeros = (0,) * len(shape)
    return pl.BlockSpec(shape, lambda *_: zeros, pipeline_mode=pl.Buffered(1))


def _params(semantics, vmem_mib):
    return pltpu.CompilerParams(dimension_semantics=semantics,
                                vmem_limit_bytes=vmem_mib * 1024 * 1024)


def _stack_heads(q2):
    lane = lax.broadcasted_iota(jnp.int32, (1, LANES), 1)
    zero = jnp.zeros_like(q2)
    return jnp.concatenate([jnp.where(lane < HEAD_DIM, q2, zero),
                            jnp.where(lane >= HEAD_DIM, q2, zero)], axis=0)


def _unstack_heads(o):
    rows = o.shape[0] // 2
    lane = lax.broadcasted_iota(jnp.int32, (1, LANES), 1)
    return jnp.where(lane < HEAD_DIM, o[:rows], o[rows:])


def _mem_kv_kernel(mem_ref, g_ref, wk_ref, wv_ref, k_ref, v_ref):
    m = _rmsnorm(mem_ref[0], g_ref[...]).astype(BF16)
    k_ref[0] = _dot(m, wk_ref[...]).astype(BF16)
    v_ref[0] = _dot(m, wv_ref[...]).astype(BF16)


def _mem_kv(mem, g_mem, w_xk, w_xv):
    B = mem.shape[0]
    blk = pl.BlockSpec((1, N_MEM, D_MODEL), lambda b: (b, 0, 0))
    return pl.pallas_call(
        _mem_kv_kernel,
        grid=(B,),
        in_specs=[blk, _resident((1, D_MODEL)), _resident((D_MODEL, D_MODEL)),
                  _resident((D_MODEL, D_MODEL))],
        out_specs=[blk, blk],
        out_shape=[jax.ShapeDtypeStruct((B, N_MEM, D_MODEL), BF16)] * 2,
        compiler_params=_params(("parallel",), 24),
        name="mem_kv",
    )(mem, g_mem, w_xk, w_xv)


def _in_proj_kernel(x_ref, g_ref, wa_ref, wk_ref, wqt_ref, wvt_ref, wg_ref,
                    qa, ka, va, qft, kf, vft, gate):
    h = _rmsnorm(x_ref[...], g_ref[...]).astype(BF16)
    scale = 1.0 / math.sqrt(HEAD_DIM)
    qa[...] = _dot(h, wa_ref[:, :DIL_WIDTH]) * scale
    ka[...] = _dot(h, wa_ref[:, DIL_WIDTH:2 * DIL_WIDTH])
    va[...] = _dot(h, wa_ref[:, 2 * DIL_WIDTH:])
    qft[...] = (_dot_nt(wqt_ref[...], h) * scale).astype(BF16)
    kf[...] = _dot(h, wk_ref[...]).astype(BF16)
    vft[...] = _dot_nt(wvt_ref[...], h).astype(BF16)
    gate[...] = _dot(h, wg_ref[...])


def _in_proj(x2d, g_mix, w_a, w_kf, w_qf_t, w_vf_t, w_gate):
    T = x2d.shape[0]
    row = lambda i: (i, 0)
    col = lambda i: (0, i)
    rows_blk = pl.BlockSpec((ROW_TILE, DIL_WIDTH), row)
    cols_blk = pl.BlockSpec((FOX_WIDTH, ROW_TILE), col)
    return pl.pallas_call(
        _in_proj_kernel,
        grid=(T // ROW_TILE,),
        in_specs=[pl.BlockSpec((ROW_TILE, D_MODEL), row), _resident((1, D_MODEL)),
                  _resident((D_MODEL, 3 * DIL_WIDTH)), _resident((D_MODEL, FOX_WIDTH)),
                  _resident((FOX_WIDTH, D_MODEL)), _resident((FOX_WIDTH, D_MODEL)),
                  _resident((D_MODEL, LANES))],
        out_specs=[rows_blk] * 3 + [cols_blk, rows_blk, cols_blk,
                                    pl.BlockSpec((ROW_TILE, LANES), row)],
        out_shape=[jax.ShapeDtypeStruct((T, DIL_WIDTH), F32)] * 3
        + [jax.ShapeDtypeStruct((FOX_WIDTH, T), BF16), jax.ShapeDtypeStruct((T, FOX_WIDTH), BF16),
           jax.ShapeDtypeStruct((FOX_WIDTH, T), BF16), jax.ShapeDtypeStruct((T, LANES), F32)],
        compiler_params=_params(("parallel",), 40),
        name="in_proj",
    )(x2d, g_mix, w_a, w_kf, w_qf_t, w_vf_t, w_gate)


def _split3(x):
    hi = x.astype(BF16)
    rest = x - hi.astype(F32)
    mid = rest.astype(BF16)
    lo = (rest - mid.astype(F32)).astype(BF16)
    return hi, mid, lo


def _gate_cum_kernel(gate_ref, b_ref, c_ref):
    r = lax.broadcasted_iota(jnp.int32, (BLOCK, BLOCK), 0)
    c = lax.broadcasted_iota(jnp.int32, (BLOCK, BLOCK), 1)
    lower = (r >= c).astype(BF16)
    is_head = lax.broadcasted_iota(jnp.int32, (1, LANES), 1) < N_FOX_HEADS
    carry = jnp.zeros((1, LANES), F32)
    for blk in range(SEQ // BLOCK):
        rows = slice(blk * BLOCK, (blk + 1) * BLOCK)
        z = gate_ref[rows, :] + b_ref[...]
        log_f = jnp.minimum(z, 0.0) - jnp.log1p(jnp.exp(-jnp.abs(z)))
        cum = sum(_dot(lower, piece) for piece in _split3(log_f)) + carry
        carry = cum[BLOCK - 1:BLOCK, :]
        packed = sum(pltpu.roll(jnp.where(is_head, piece.astype(F32), 0.0), N_FOX_HEADS * i, axis=1)
                     if i else jnp.where(is_head, piece.astype(F32), 0.0)
                     for i, piece in enumerate(_split3(cum)))
        c_ref[0, rows, :] = packed.astype(BF16)


def _gate_cum(gate, b_forget, B):
    return pl.pallas_call(
        _gate_cum_kernel,
        grid=(B,),
        in_specs=[pl.BlockSpec((SEQ, LANES), lambda b: (b, 0)), _resident((1, LANES))],
        out_specs=pl.BlockSpec((1, SEQ, LANES), lambda b: (b, 0, 0)),
        out_shape=jax.ShapeDtypeStruct((B, SEQ, LANES), BF16),
        compiler_params=_params(("parallel",), 16),
        name="gate_cum",
    )(gate, b_forget)


def _dilated_kernel(slopes_ref, q_ref, k_ref, v_ref, o_ref, ob_ref, lb_ref):
    pair = pl.program_id(1)
    row = lax.broadcasted_iota(jnp.int32, (2 * BLOCK, 1), 0)
    slope = jnp.where(row < BLOCK, slopes_ref[2 * pair], slopes_ref[2 * pair + 1])
    qi = jnp.where(row < BLOCK, row, row - BLOCK)
    lane = lax.broadcasted_iota(jnp.int32, (1, LANES), 1)

    def rows(start, size, dilation):
        if dilation == 1:
            return pl.ds(start, size)
        return pl.ds(start, size, stride=dilation)

    for branch, (window, dilation) in enumerate(DIL_CONFIGS):
        steps = window // dilation
        length = SEQ // dilation
        n_blocks = length // BLOCK

        def bias_table(n_keys, first_key):
            kj = lax.broadcasted_iota(jnp.int32, (1, n_keys), 1) + first_key
            delta = qi - kj
            valid = (delta >= 0) & (delta <= steps)
            dist = (delta * dilation).astype(F32)
            return jnp.where(valid, -slope * dist, NEG)

        bias_first = bias_table(BLOCK, 0)
        bias_rest = bias_table(2 * BLOCK, -BLOCK)

        def attend(q_start, k_start, n_keys, bias):
            q2 = q_ref[0, rows(q_start, BLOCK, dilation), :].astype(BF16)
            k2 = k_ref[0, rows(k_start, n_keys, dilation), :].astype(BF16)
            v2 = v_ref[0, rows(k_start, n_keys, dilation), :].astype(BF16)
            s = _dot_nt(_stack_heads(q2), k2) + bias
            m = jnp.max(s, axis=-1, keepdims=True)
            p = jnp.exp(s - m)
            l = jnp.sum(p, axis=-1, keepdims=True)
            o = _dot(p.astype(BF16), v2) / l
            lse = jnp.broadcast_to(m + jnp.log(l), (2 * BLOCK, LANES))
            ob_ref[branch, rows(q_start, BLOCK, dilation), :] = _unstack_heads(o)
            lb_ref[branch, rows(q_start, BLOCK, dilation), :] = _unstack_heads(lse)

        def first_block(r):
            attend(r, r, BLOCK, bias_first)

        def later_block(r, n):
            attend(r + n * (dilation * BLOCK), r + (n - 1) * (dilation * BLOCK),
                   2 * BLOCK, bias_rest)

        if n_blocks == 1:
            def residue_group(g, carry):
                for u in range(DIL_GROUP):
                    first_block(g * DIL_GROUP + u)
                return carry

            lax.fori_loop(0, dilation // DIL_GROUP, residue_group, 0)
        elif n_blocks <= DIL_GROUP:
            def whole_residue(r, carry):
                first_block(r)
                for n in range(1, n_blocks):
                    later_block(r, n)
                return carry

            lax.fori_loop(0, dilation, whole_residue, 0)
        else:
            group = max(g for g in range(1, DIL_GROUP + 1) if (n_blocks - 1) % g == 0)
            for r in range(dilation):
                first_block(r)

                def block_group(i, carry, r=r):
                    for u in range(group):
                        later_block(r, 1 + i * group + u)
                    return carry

                lax.fori_loop(0, (n_blocks - 1) // group, block_group, 0)

    def mix(i, carry):
        chunk = pl.ds(pl.multiple_of(i * MIX_CHUNK, MIX_CHUNK), MIX_CHUNK)
        lses = [lb_ref[b, chunk, :] for b in range(len(DIL_CONFIGS))]
        top = functools.reduce(jnp.maximum, lses)
        ws = [jnp.exp(l - top) for l in lses]
        num = sum(w * ob_ref[b, chunk, :] for b, w in enumerate(ws))
        o_ref[0, chunk, :] = (num / sum(ws)).astype(o_ref.dtype)
        return carry

    lax.fori_loop(0, SEQ // MIX_CHUNK, mix, 0)


def _dilated(slopes, qa, ka, va):
    B = qa.shape[0]
    blk = pl.BlockSpec((1, SEQ, LANES), lambda b, p: (b, 0, p))
    n_br = len(DIL_CONFIGS)
    return pl.pallas_call(
        _dilated_kernel,
        grid=(B, HEAD_PAIRS),
        in_specs=[pl.BlockSpec(memory_space=pltpu.SMEM), blk, blk, blk],
        out_specs=blk,
        out_shape=jax.ShapeDtypeStruct((B, SEQ, DIL_WIDTH), BF16),
        scratch_shapes=[pltpu.VMEM((n_br, SEQ, LANES), F32),
                        pltpu.VMEM((n_br, SEQ, LANES), F32)],
        compiler_params=_params(("parallel", "parallel"), 32),
        name="dilated",
    )(slopes, qa, ka, va)


def _fox_kernel(qt_ref, k_ref, c_ref, vt_ref, o_ref,
                s_ref, p_ref, m_ref, alpha_ref, l_ref, acc_ref):
    first_head = pl.program_id(1) * FOX_HEADS_PER_STEP
    heads = range(FOX_HEADS_PER_STEP)
    tile = FOX_TILE
    kpos = lax.broadcasted_iota(jnp.int32, (tile, tile), 0)
    qpos = lax.broadcasted_iota(jnp.int32, (tile, tile), 1)
    causal = kpos <= qpos
    chan = lax.broadcasted_iota(jnp.int32, (LANES, tile), 0)

    def minus_c(head):
        hit = functools.reduce(jnp.logical_or,
                               [chan == head + N_FOX_HEADS * i for i in range(3)])
        return jnp.where(hit, -1.0, 0.0).astype(BF16)

    pick_c = [minus_c(first_head + h) for h in heads]
    own = [chan < HEAD_DIM, chan >= HEAD_DIM]

    def per_query_tile(i, carry):
        q_cols = pl.ds(pl.multiple_of(i * tile, tile), tile)
        rhs = []
        for h in heads:
            qt = qt_ref[(h // 2) * LANES:(h // 2 + 1) * LANES, q_cols]
            rhs.append(jnp.concatenate(
                [jnp.where(own[h % 2], qt, jnp.zeros_like(qt)), pick_c[h]], axis=0))

        def key_rows(j):
            return pl.ds(pl.multiple_of(j * tile, tile), tile)

        def scores(j):
            c_k = c_ref[0, key_rows(j), :]
            out = []
            for h in heads:
                k2 = k_ref[0, key_rows(j), (h // 2) * LANES:(h // 2 + 1) * LANES]
                out.append(_dot(jnp.concatenate([k2, c_k], axis=1), rhs[h]))
            return tuple(out)

        def store_scores(j):
            for h, s in enumerate(scores(j)):
                s_ref[h] = s

        def accumulate(j):
            for h in heads:
                vt = vt_ref[h * HEAD_DIM:(h + 1) * HEAD_DIM, key_rows(j)]
                acc_ref[h] = alpha_ref[h] * acc_ref[h] + _dot(vt, p_ref[h])

        def softmax_step(diagonal):
            for h in heads:
                s = jnp.where(causal, s_ref[h], NEG) if diagonal else s_ref[h]
                m = m_ref[h]
                m_new = jnp.maximum(m, jnp.max(s, axis=0, keepdims=True))
                alpha = jnp.exp(m - m_new)
                p = jnp.exp(s - m_new)
                m_ref[h] = m_new
                alpha_ref[h] = alpha
                l_ref[h] = alpha * l_ref[h] + jnp.sum(p, axis=0, keepdims=True)
                p_ref[h] = p.astype(BF16)

        def full_tile(j, carry):
            accumulate(jnp.maximum(j - 1, 0))
            softmax_step(diagonal=False)
            store_scores(j + 1)
            return carry

        for h in heads:
            m_ref[h] = jnp.full((1, tile), NEG, F32)
            l_ref[h] = jnp.zeros((1, tile), F32)
            acc_ref[h] = jnp.zeros((HEAD_DIM, tile), F32)
            alpha_ref[h] = jnp.ones((1, tile), F32)
            p_ref[h] = jnp.zeros((tile, tile), BF16)
        store_scores(0)
        lax.fori_loop(0, i, full_tile, 0)
        accumulate(jnp.maximum(i - 1, 0))
        softmax_step(diagonal=True)
        accumulate(i)
        o_t = jnp.concatenate([acc_ref[h] / l_ref[h] for h in heads], axis=0)
        o_ref[0, q_cols, :] = o_t.T.astype(o_ref.dtype)
        return carry

    lax.fori_loop(0, SEQ // tile, per_query_tile, 0)


def _fox(qf_t, kf, c_pieces, vf_t):
    B = kf.shape[0]
    width = FOX_HEADS_PER_STEP * HEAD_DIM
    rows_blk = pl.BlockSpec((1, SEQ, width), lambda b, g: (b, 0, g))
    cols_blk = pl.BlockSpec((width, SEQ), lambda b, g: (g, b))
    return pl.pallas_call(
        _fox_kernel,
        grid=(B, N_FOX_HEADS // FOX_HEADS_PER_STEP),
        in_specs=[cols_blk, rows_blk, pl.BlockSpec((1, SEQ, LANES), lambda b, g: (b, 0, 0)),
                  cols_blk],
        out_specs=rows_blk,
        out_shape=jax.ShapeDtypeStruct((B, SEQ, FOX_WIDTH), BF16),
        scratch_shapes=[pltpu.VMEM((FOX_HEADS_PER_STEP, FOX_TILE, FOX_TILE), F32),
                        pltpu.VMEM((FOX_HEADS_PER_STEP, FOX_TILE, FOX_TILE), BF16),
                        pltpu.VMEM((FOX_HEADS_PER_STEP, 1, FOX_TILE), F32),
                        pltpu.VMEM((FOX_HEADS_PER_STEP, 1, FOX_TILE), F32),
                        pltpu.VMEM((FOX_HEADS_PER_STEP, 1, FOX_TILE), F32),
                        pltpu.VMEM((FOX_HEADS_PER_STEP, HEAD_DIM, FOX_TILE), F32)],
        compiler_params=_params(("parallel", "parallel"), 32),
        name="fox",
    )(qf_t, kf, c_pieces, vf_t)


def _post_mix_kernel(x_ref, ya_ref, yf_ref, km_ref, vm_ref, woa_ref, wob_ref, g_ref,
                     wq_ref, wo_ref, o_ref):
    x = x_ref[0] + _dot(ya_ref[0], woa_ref[...]) + _dot(yf_ref[0], wob_ref[...])
    h = _rmsnorm(x, g_ref[...]).astype(BF16)
    scale = 1.0 / math.sqrt(XATTN_HEAD_DIM)
    q = (_dot(h, wq_ref[...]) * scale).astype(BF16)
    heads = []
    for hd in range(N_XATTN_HEADS):
        cols = slice(hd * XATTN_HEAD_DIM, (hd + 1) * XATTN_HEAD_DIM)
        s = _dot_nt(q[:, cols], km_ref[0, :, cols])
        p = jnp.exp(s - jnp.max(s, axis=-1, keepdims=True))
        l = jnp.sum(p, axis=-1, keepdims=True)
        heads.append((_dot(p.astype(BF16), vm_ref[0, :, cols]) / l).astype(BF16))
    o_ref[0] = x + _dot(jnp.concatenate(heads, axis=-1), wo_ref[...])


def _post_mix(x, ya, yf, km, vm, w_oa, w_ob, g_xattn, w_xq, w_xo):
    B = x.shape[0]
    tok = lambda b, i: (b, i, 0)
    per_b = lambda b, i: (b, 0, 0)
    return pl.pallas_call(
        _post_mix_kernel,
        grid=(B, SEQ // ROW_TILE),
        in_specs=[pl.BlockSpec((1, ROW_TILE, D_MODEL), tok),
                  pl.BlockSpec((1, ROW_TILE, DIL_WIDTH), tok),
                  pl.BlockSpec((1, ROW_TILE, FOX_WIDTH), tok),
                  pl.BlockSpec((1, N_MEM, D_MODEL), per_b),
                  pl.BlockSpec((1, N_MEM, D_MODEL), per_b),
                  _resident((DIL_WIDTH, D_MODEL)), _resident((FOX_WIDTH, D_MODEL)),
                  _resident((1, D_MODEL)), _resident((D_MODEL, D_MODEL)),
                  _resident((D_MODEL, D_MODEL))],
        out_specs=pl.BlockSpec((1, ROW_TILE, D_MODEL), tok),
        out_shape=jax.ShapeDtypeStruct(x.shape, F32),
        compiler_params=_params(("parallel", "parallel"), 40),
        name="post_mix",
    )(x, ya, yf, km, vm, w_oa, w_ob, g_xattn, w_xq, w_xo)


def _mlp_kernel(x_ref, g_ref, wu_ref, wd_ref, gf_ref, o_ref):
    x = x_ref[...]
    h = _rmsnorm(x, g_ref[...]).astype(BF16)
    y = x
    for c in range(D_FF // FF_CHUNK):
        cols = slice(c * FF_CHUNK, (c + 1) * FF_CHUNK)
        a = jnp.maximum(_dot(h, wu_ref[:, cols]), 0.0)
        y = y + _dot((a * a).astype(BF16), wd_ref[cols, :])
    o_ref[...] = _rmsnorm(y, gf_ref[...])


def _mlp(x2d, g_mlp, w_up, w_down, g_final):
    T = x2d.shape[0]
    row = pl.BlockSpec((ROW_TILE, D_MODEL), lambda i: (i, 0))
    return pl.pallas_call(
        _mlp_kernel,
        grid=(T // ROW_TILE,),
        in_specs=[row, _resident((1, D_MODEL)), _resident((D_MODEL, D_FF)),
                  _resident((D_FF, D_MODEL)), _resident((1, D_MODEL))],
        out_specs=row,
        out_shape=jax.ShapeDtypeStruct(x2d.shape, F32),
        compiler_params=_params(("parallel",), 48),
        name="mlp",
    )(x2d, g_mlp, w_up, w_down, g_final)


def kernel(x, mem, g_mix, w_in, b_forget, w_out, g_xattn, g_mem, w_xq, w_xk, w_xv, w_xo,
           g_mlp, w_up, w_down, g_final):
    B, S, D = x.shape
    assert (S, D) == (SEQ, D_MODEL) and mem.shape == (B, N_MEM, D_MODEL)
    gain = lambda g: g.reshape(1, D_MODEL).astype(F32)
    bf = lambda w: w.astype(BF16)
    slopes = 2.0 ** (-(jnp.arange(1, N_DIL_HEADS + 1, dtype=F32) * (8.0 / N_DIL_HEADS)))

    km, vm = _mem_kv(mem, gain(g_mem), bf(w_xk), bf(w_xv))
    o_qf = 3 * DIL_WIDTH
    o_kf, o_vf = o_qf + FOX_WIDTH, o_qf + 2 * FOX_WIDTH
    pad_heads = lambda a: jnp.pad(a, ((0, 0), (0, LANES - N_FOX_HEADS)))
    qa, ka, va, qf_t, kf, vf_t, gate = _in_proj(
        x.reshape(B * S, D), gain(g_mix), bf(w_in[:, :o_qf]), bf(w_in[:, o_kf:o_vf]),
        bf(w_in[:, o_qf:o_kf].T), bf(w_in[:, o_vf:QKV_WIDTH].T), bf(pad_heads(w_in[:, QKV_WIDTH:])))
    c_pieces = _gate_cum(gate, pad_heads(b_forget.reshape(1, N_FOX_HEADS).astype(F32)), B)
    seq = lambda a: a.reshape(B, S, a.shape[-1])
    ya = _dilated(slopes, seq(qa), seq(ka), seq(va))
    yf = _fox(qf_t, seq(kf), c_pieces, vf_t)
    x = _post_mix(x, ya, yf, km, vm, bf(w_out[:DIL_WIDTH]), bf(w_out[DIL_WIDTH:]),
                  gain(g_xattn), bf(w_xq), bf(w_xo))
    y = _mlp(x.reshape(B * S, D), gain(g_mlp), bf(w_up), bf(w_down), gain(g_final))
    return y.reshape(B, S, D)
```

```python
import functools
import math

import jax
import jax.numpy as jnp
from jax import lax
from jax.experimental import pallas as pl
from jax.experimental.pallas import tpu as pltpu

D_MODEL = 1024
SEQ = 2048
N_MEM = 256
HEAD_DIM = 64
N_DIL_HEADS = 8
N_FOX_HEADS = 8
DIL_WIDTH = N_DIL_HEADS * HEAD_DIM
FOX_WIDTH = N_FOX_HEADS * HEAD_DIM
QKV_WIDTH = 3 * DIL_WIDTH + 3 * FOX_WIDTH
DIL_CONFIGS = ((128, 1), (512, 4), (2048, 16))
BLOCK = 128
N_XATTN_HEADS = 4
XATTN_HEAD_DIM = D_MODEL // N_XATTN_HEADS
D_FF = 4 * D_MODEL
EPS = 1e-6
NEG = -1e30

LANES = 128
HEAD_PAIRS = N_FOX_HEADS // 2
ROW_TILE = 512
FOX_TILE = 256
FOX_HEADS_PER_STEP = 8
FF_CHUNK = 1024
MIX_CHUNK = 256
DIL_GROUP = 16

F32 = jnp.float32
BF16 = jnp.bfloat16

_NT = (((1,), (1,)), ((), ()))


def _dot(a, b):
    return jnp.dot(a, b, preferred_element_type=F32)


def _dot_nt(a, b):
    return lax.dot_general(a, b, _NT, preferred_element_type=F32)


def _rmsnorm(x, g):
    return x * lax.rsqrt(jnp.mean(x * x, axis=-1, keepdims=True) + EPS) * g


def _resident(shape):
    zeros = (0,) * len(shape)
    return pl.BlockSpec(shape, lambda *_: zeros, pipeline_mode=pl.Buffered(1))


def _params(semantics, vmem_mib):
    return pltpu.CompilerParams(dimension_semantics=semantics,
                                vmem_limit_bytes=vmem_mib * 1024 * 1024)


def _stack_heads(q2):
    lane = lax.broadcasted_iota(jnp.int32, (1, LANES), 1)
    zero = jnp.zeros_like(q2)
    return jnp.concatenate([jnp.where(lane < HEAD_DIM, q2, zero),
                            jnp.where(lane >= HEAD_DIM, q2, zero)], axis=0)


def _unstack_heads(o):
    rows = o.shape[0] // 2
    lane = lax.broadcasted_iota(jnp.int32, (1, LANES), 1)
    return jnp.where(lane < HEAD_DIM, o[:rows], o[rows:])


def _mem_kv_kernel(mem_ref, g_ref, wk_ref, wv_ref, k_ref, v_ref):
    m = _rmsnorm(mem_ref[0], g_ref[...]).astype(BF16)
    k_ref[0] = _dot(m, wk_ref[...]).astype(BF16)
    v_ref[0] = _dot(m, wv_ref[...]).astype(BF16)


def _mem_kv(mem, g_mem, w_xk, w_xv):
    B = mem.shape[0]
    blk = pl.BlockSpec((1, N_MEM, D_MODEL), lambda b: (b, 0, 0))
    return pl.pallas_call(
        _mem_kv_kernel,
        grid=(B,),
        in_specs=[blk, _resident((1, D_MODEL)), _resident((D_MODEL, D_MODEL)),
                  _resident((D_MODEL, D_MODEL))],
        out_specs=[blk, blk],
        out_shape=[jax.ShapeDtypeStruct((B, N_MEM, D_MODEL), BF16)] * 2,
        compiler_params=_params(("parallel",), 24),
        name="mem_kv",
    )(mem, g_mem, w_xk, w_xv)


def _in_proj_kernel(x_ref, g_ref, wa_ref, wk_ref, wqt_ref, wvt_ref, wg_ref,
                    qa, ka, va, qft, kf, vft, gate):
    h = _rmsnorm(x_ref[...], g_ref[...]).astype(BF16)
    scale = 1.0 / math.sqrt(HEAD_DIM)
    qa[...] = _dot(h, wa_ref[:, :DIL_WIDTH]) * scale
    ka[...] = _dot(h, wa_ref[:, DIL_WIDTH:2 * DIL_WIDTH])
    va[...] = _dot(h, wa_ref[:, 2 * DIL_WIDTH:])
    qft[...] = (_dot_nt(wqt_ref[...], h) * scale).astype(BF16)
    kf[...] = _dot(h, wk_ref[...]).astype(BF16)
    vft[...] = _dot_nt(wvt_ref[...], h).astype(BF16)
    gate[...] = _dot(h, wg_ref[...])


def _in_proj(x2d, g_mix, w_a, w_kf, w_qf_t, w_vf_t, w_gate):
    T = x2d.shape[0]
    row = lambda i: (i, 0)
    col = lambda i: (0, i)
    rows_blk = pl.BlockSpec((ROW_TILE, DIL_WIDTH), row)
    cols_blk = pl.BlockSpec((FOX_WIDTH, ROW_TILE), col)
    return pl.pallas_call(
        _in_proj_kernel,
        grid=(T // ROW_TILE,),
        in_specs=[pl.BlockSpec((ROW_TILE, D_MODEL), row), _resident((1, D_MODEL)),
                  _resident((D_MODEL, 3 * DIL_WIDTH)), _resident((D_MODEL, FOX_WIDTH)),
                  _resident((FOX_WIDTH, D_MODEL)), _resident((FOX_WIDTH, D_MODEL)),
                  _resident((D_MODEL, LANES))],
        out_specs=[rows_blk] * 3 + [cols_blk, rows_blk, cols_blk,
                                    pl.BlockSpec((ROW_TILE, LANES), row)],
        out_shape=[jax.ShapeDtypeStruct((T, DIL_WIDTH), F32)] * 3
        + [jax.ShapeDtypeStruct((FOX_WIDTH, T), BF16), jax.ShapeDtypeStruct((T, FOX_WIDTH), BF16),
           jax.ShapeDtypeStruct((FOX_WIDTH, T), BF16), jax.ShapeDtypeStruct((T, LANES), F32)],
        compiler_params=_params(("parallel",), 40),
        name="in_proj",
    )(x2d, g_mix, w_a, w_kf, w_qf_t, w_vf_t, w_gate)


def _split3(x):
    hi = x.astype(BF16)
    rest = x - hi.astype(F32)
    mid = rest.astype(BF16)
    lo = (rest - mid.astype(F32)).astype(BF16)
    return hi, mid, lo


def _gate_cum_kernel(gate_ref, b_ref, c_ref):
    r = lax.broadcasted_iota(jnp.int32, (BLOCK, BLOCK), 0)
    c = lax.broadcasted_iota(jnp.int32, (BLOCK, BLOCK), 1)
    lower = (r >= c).astype(BF16)
    is_head = lax.broadcasted_iota(jnp.int32, (1, LANES), 1) < N_FOX_HEADS
    carry = jnp.zeros((1, LANES), F32)
    for blk in range(SEQ // BLOCK):
        rows = slice(blk * BLOCK, (blk + 1) * BLOCK)
        z = gate_ref[rows, :] + b_ref[...]
        log_f = jnp.minimum(z, 0.0) - jnp.log1p(jnp.exp(-jnp.abs(z)))
        cum = sum(_dot(lower, piece) for piece in _split3(log_f)) + carry
        carry = cum[BLOCK - 1:BLOCK, :]
        packed = sum(pltpu.roll(jnp.where(is_head, piece.astype(F32), 0.0), N_FOX_HEADS * i, axis=1)
                     if i else jnp.where(is_head, piece.astype(F32), 0.0)
                     for i, piece in enumerate(_split3(cum)))
        c_ref[0, rows, :] = packed.astype(BF16)


def _gate_cum(gate, b_forget, B):
    return pl.pallas_call(
        _gate_cum_kernel,
        grid=(B,),
        in_specs=[pl.BlockSpec((SEQ, LANES), lambda b: (b, 0)), _resident((1, LANES))],
        out_specs=pl.BlockSpec((1, SEQ, LANES), lambda b: (b, 0, 0)),
        out_shape=jax.ShapeDtypeStruct((B, SEQ, LANES), BF16),
        compiler_params=_params(("parallel",), 16),
        name="gate_cum",
    )(gate, b_forget)


def _dilated_kernel(slopes_ref, q_ref, k_ref, v_ref, o_ref, ob_ref, lb_ref):
    pair = pl.program_id(1)
    row = lax.broadcasted_iota(jnp.int32, (2 * BLOCK, 1), 0)
    slope = jnp.where(row < BLOCK, slopes_ref[2 * pair], slopes_ref[2 * pair + 1])
    qi = jnp.where(row < BLOCK, row, row - BLOCK)
    lane = lax.broadcasted_iota(jnp.int32, (1, LANES), 1)

    def rows(start, size, dilation):
        if dilation == 1:
            return pl.ds(start, size)
        return pl.ds(start, size, stride=dilation)

    for branch, (window, dilation) in enumerate(DIL_CONFIGS):
        steps = window // dilation
        length = SEQ // dilation
        n_blocks = length // BLOCK

        def bias_table(n_keys, first_key):
            kj = lax.broadcasted_iota(jnp.int32, (1, n_keys), 1) + first_key
            delta = qi - kj
            valid = (delta >= 0) & (delta <= steps)
            dist = (delta * dilation).astype(F32)
            return jnp.where(valid, -slope * dist, NEG)

        bias_first = bias_table(BLOCK, 0)
        bias_rest = bias_table(2 * BLOCK, -BLOCK)

        def attend(q_start, k_start, n_keys, bias):
            q2 = q_ref[0, rows(q_start, BLOCK, dilation), :].astype(BF16)
            k2 = k_ref[0, rows(k_start, n_keys, dilation), :].astype(BF16)
            v2 = v_ref[0, rows(k_start, n_keys, dilation), :].astype(BF16)
            s = _dot_nt(_stack_heads(q2), k2) + bias
            m = jnp.max(s, axis=-1, keepdims=True)
            p = jnp.exp(s - m)
            l = jnp.sum(p, axis=-1, keepdims=True)
            o = _dot(p.astype(BF16), v2) / l
            lse = jnp.broadcast_to(m + jnp.log(l), (2 * BLOCK, LANES))
            ob_ref[branch, rows(q_start, BLOCK, dilation), :] = _unstack_heads(o)
            lb_ref[branch, rows(q_start, BLOCK, dilation), :] = _unstack_heads(lse)

        def first_block(r):
            attend(r, r, BLOCK, bias_first)

        def later_block(r, n):
            attend(r + n * (dilation * BLOCK), r + (n - 1) * (dilation * BLOCK),
                   2 * BLOCK, bias_rest)

        if n_blocks == 1:
            def residue_group(g, carry):
                for u in range(DIL_GROUP):
                    first_block(g * DIL_GROUP + u)
                return carry

            lax.fori_loop(0, dilation // DIL_GROUP, residue_group, 0)
        elif n_blocks <= DIL_GROUP:
            per_body = DIL_GROUP // n_blocks

            def whole_residues(g, carry):
                for u in range(per_body):
                    first_block(g * per_body + u)
                    for n in range(1, n_blocks):
                        later_block(g * per_body + u, n)
                return carry

            lax.fori_loop(0, dilation // per_body, whole_residues, 0)
        else:
            group = max(g for g in range(1, DIL_GROUP + 1) if (n_blocks - 1) % g == 0)
            for r in range(dilation):
                first_block(r)

                def block_group(i, carry, r=r):
                    for u in range(group):
                        later_block(r, 1 + i * group + u)
                    return carry

                lax.fori_loop(0, (n_blocks - 1) // group, block_group, 0)

    def mix(i, carry):
        chunk = pl.ds(pl.multiple_of(i * MIX_CHUNK, MIX_CHUNK), MIX_CHUNK)
        lses = [lb_ref[b, chunk, :] for b in range(len(DIL_CONFIGS))]
        top = functools.reduce(jnp.maximum, lses)
        ws = [jnp.exp(l - top) for l in lses]
        num = sum(w * ob_ref[b, chunk, :] for b, w in enumerate(ws))
        o_ref[0, chunk, :] = (num / sum(ws)).astype(o_ref.dtype)
        return carry

    lax.fori_loop(0, SEQ // MIX_CHUNK, mix, 0)


def _dilated(slopes, qa, ka, va):
    B = qa.shape[0]
    blk = pl.BlockSpec((1, SEQ, LANES), lambda b, p: (b, 0, p))
    n_br = len(DIL_CONFIGS)
    return pl.pallas_call(
        _dilated_kernel,
        grid=(B, HEAD_PAIRS),
        in_specs=[pl.BlockSpec(memory_space=pltpu.SMEM), blk, blk, blk],
        out_specs=blk,
        out_shape=jax.ShapeDtypeStruct((B, SEQ, DIL_WIDTH), BF16),
        scratch_shapes=[pltpu.VMEM((n_br, SEQ, LANES), F32),
                        pltpu.VMEM((n_br, SEQ, LANES), F32)],
        compiler_params=_params(("parallel", "parallel"), 32),
        name="dilated",
    )(slopes, qa, ka, va)


def _fox_kernel(qt_ref, k_ref, c_ref, vt_ref, o_ref,
                s_ref, p_ref, m_ref, alpha_ref, l_ref, acc_ref):
    first_head = pl.program_id(1) * FOX_HEADS_PER_STEP
    heads = range(FOX_HEADS_PER_STEP)
    tile = FOX_TILE
    kpos = lax.broadcasted_iota(jnp.int32, (tile, tile), 0)
    qpos = lax.broadcasted_iota(jnp.int32, (tile, tile), 1)
    causal = kpos <= qpos
    chan = lax.broadcasted_iota(jnp.int32, (LANES, tile), 0)

    def minus_c(head):
        hit = functools.reduce(jnp.logical_or,
                               [chan == head + N_FOX_HEADS * i for i in range(3)])
        return jnp.where(hit, -1.0, 0.0).astype(BF16)

    pick_c = [minus_c(first_head + h) for h in heads]
    own = [chan < HEAD_DIM, chan >= HEAD_DIM]

    def per_query_tile(i, carry):
        q_cols = pl.ds(pl.multiple_of(i * tile, tile), tile)
        rhs = []
        for h in heads:
            qt = qt_ref[(h // 2) * LANES:(h // 2 + 1) * LANES, q_cols]
            rhs.append(jnp.concatenate(
                [jnp.where(own[h % 2], qt, jnp.zeros_like(qt)), pick_c[h]], axis=0))

        def key_rows(j):
            return pl.ds(pl.multiple_of(j * tile, tile), tile)

        def scores(j):
            c_k = c_ref[0, key_rows(j), :]
            out = []
            for h in heads:
                k2 = k_ref[0, key_rows(j), (h // 2) * LANES:(h // 2 + 1) * LANES]
                out.append(_dot(jnp.concatenate([k2, c_k], axis=1), rhs[h]))
            return tuple(out)

        def store_scores(j):
            for h, s in enumerate(scores(j)):
                s_ref[h] = s

        def accumulate(j):
            for h in heads:
                vt = vt_ref[h * HEAD_DIM:(h + 1) * HEAD_DIM, key_rows(j)]
                acc_ref[h] = alpha_ref[h] * acc_ref[h] + _dot(vt, p_ref[h])

        def softmax_step(diagonal):
            for h in heads:
                s = jnp.where(causal, s_ref[h], NEG) if diagonal else s_ref[h]
                m = m_ref[h]
                m_new = jnp.maximum(m, jnp.max(s, axis=0, keepdims=True))
                alpha = jnp.exp(m - m_new)
                p = jnp.exp(s - m_new)
                m_ref[h] = m_new
                alpha_ref[h] = alpha
                l_ref[h] = alpha * l_ref[h] + jnp.sum(p, axis=0, keepdims=True)
                p_ref[h] = p.astype(BF16)

        def full_tile(j, carry):
            accumulate(jnp.maximum(j - 1, 0))
            softmax_step(diagonal=False)
            store_scores(j + 1)
            return carry

        for h in heads:
            m_ref[h] = jnp.full((1, tile), NEG, F32)
            l_ref[h] = jnp.zeros((1, tile), F32)
            acc_ref[h] = jnp.zeros((HEAD_DIM, tile), F32)
            alpha_ref[h] = jnp.ones((1, tile), F32)
            p_ref[h] = jnp.zeros((tile, tile), BF16)
        store_scores(0)
        lax.fori_loop(0, i, full_tile, 0)
        accumulate(jnp.maximum(i - 1, 0))
        softmax_step(diagonal=True)
        accumulate(i)
        o_t = jnp.concatenate([acc_ref[h] / l_ref[h] for h in heads], axis=0)
        o_ref[0, q_cols, :] = o_t.T.astype(o_ref.dtype)
        return carry

    lax.fori_loop(0, SEQ // tile, per_query_tile, 0)


def _fox(qf_t, kf, c_pieces, vf_t):
    B = kf.shape[0]
    width = FOX_HEADS_PER_STEP * HEAD_DIM
    rows_blk = pl.BlockSpec((1, SEQ, width), lambda b, g: (b, 0, g))
    cols_blk = pl.BlockSpec((width, SEQ), lambda b, g: (g, b))
    return pl.pallas_call(
        _fox_kernel,
        grid=(B, N_FOX_HEADS // FOX_HEADS_PER_STEP),
        in_specs=[cols_blk, rows_blk, pl.BlockSpec((1, SEQ, LANES), lambda b, g: (b, 0, 0)),
                  cols_blk],
        out_specs=rows_blk,
        out_shape=jax.ShapeDtypeStruct((B, SEQ, FOX_WIDTH), BF16),
        scratch_shapes=[pltpu.VMEM((FOX_HEADS_PER_STEP, FOX_TILE, FOX_TILE), F32),
                        pltpu.VMEM((FOX_HEADS_PER_STEP, FOX_TILE, FOX_TILE), BF16),
                        pltpu.VMEM((FOX_HEADS_PER_STEP, 1, FOX_TILE), F32),
                        pltpu.VMEM((FOX_HEADS_PER_STEP, 1, FOX_TILE), F32),
                        pltpu.VMEM((FOX_HEADS_PER_STEP, 1, FOX_TILE), F32),
                        pltpu.VMEM((FOX_HEADS_PER_STEP, HEAD_DIM, FOX_TILE), F32)],
        compiler_params=_params(("parallel", "parallel"), 32),
        name="fox",
    )(qf_t, kf, c_pieces, vf_t)


def _post_mix_kernel(x_ref, ya_ref, yf_ref, km_ref, vm_ref, woa_ref, wob_ref, g_ref,
                     wq_ref, wo_ref, o_ref):
    x = x_ref[0] + _dot(ya_ref[0], woa_ref[...]) + _dot(yf_ref[0], wob_ref[...])
    h = _rmsnorm(x, g_ref[...]).astype(BF16)
    scale = 1.0 / math.sqrt(XATTN_HEAD_DIM)
    q = (_dot(h, wq_ref[...]) * scale).astype(BF16)
    heads = []
    for hd in range(N_XATTN_HEADS):
        cols = slice(hd * XATTN_HEAD_DIM, (hd + 1) * XATTN_HEAD_DIM)
        s = _dot_nt(q[:, cols], km_ref[0, :, cols])
        p = jnp.exp(s - jnp.max(s, axis=-1, keepdims=True))
        l = jnp.sum(p, axis=-1, keepdims=True)
        heads.append((_dot(p.astype(BF16), vm_ref[0, :, cols]) / l).astype(BF16))
    o_ref[0] = x + _dot(jnp.concatenate(heads, axis=-1), wo_ref[...])


def _post_mix(x, ya, yf, km, vm, w_oa, w_ob, g_xattn, w_xq, w_xo):
    B = x.shape[0]
    tok = lambda b, i: (b, i, 0)
    per_b = lambda b, i: (b, 0, 0)
    return pl.pallas_call(
        _post_mix_kernel,
        grid=(B, SEQ // ROW_TILE),
        in_specs=[pl.BlockSpec((1, ROW_TILE, D_MODEL), tok),
                  pl.BlockSpec((1, ROW_TILE, DIL_WIDTH), tok),
                  pl.BlockSpec((1, ROW_TILE, FOX_WIDTH), tok),
                  pl.BlockSpec((1, N_MEM, D_MODEL), per_b),
                  pl.BlockSpec((1, N_MEM, D_MODEL), per_b),
                  _resident((DIL_WIDTH, D_MODEL)), _resident((FOX_WIDTH, D_MODEL)),
                  _resident((1, D_MODEL)), _resident((D_MODEL, D_MODEL)),
                  _resident((D_MODEL, D_MODEL))],
        out_specs=pl.BlockSpec((1, ROW_TILE, D_MODEL), tok),
        out_shape=jax.ShapeDtypeStruct(x.shape, F32),
        compiler_params=_params(("parallel", "parallel"), 40),
        name="post_mix",
    )(x, ya, yf, km, vm, w_oa, w_ob, g_xattn, w_xq, w_xo)


def _mlp_kernel(x_ref, g_ref, wu_ref, wd_ref, gf_ref, o_ref):
    x = x_ref[...]
    h = _rmsnorm(x, g_ref[...]).astype(BF16)
    y = x
    for c in range(D_FF // FF_CHUNK):
        cols = slice(c * FF_CHUNK, (c + 1) * FF_CHUNK)
        a = jnp.maximum(_dot(h, wu_ref[:, cols]), 0.0)
        y = y + _dot((a * a).astype(BF16), wd_ref[cols, :])
    o_ref[...] = _rmsnorm(y, gf_ref[...])


def _mlp(x2d, g_mlp, w_up, w_down, g_final):
    T = x2d.shape[0]
    row = pl.BlockSpec((ROW_TILE, D_MODEL), lambda i: (i, 0))
    return pl.pallas_call(
        _mlp_kernel,
        grid=(T // ROW_TILE,),
        in_specs=[row, _resident((1, D_MODEL)), _resident((D_MODEL, D_FF)),
                  _resident((D_FF, D_MODEL)), _resident((1, D_MODEL))],
        out_specs=row,
        out_shape=jax.ShapeDtypeStruct(x2d.shape, F32),
        compiler_params=_params(("parallel",), 48),
        name="mlp",
    )(x2d, g_mlp, w_up, w_down, g_final)


def kernel(x, mem, g_mix, w_in, b_forget, w_out, g_xattn, g_mem, w_xq, w_xk, w_xv, w_xo,
           g_mlp, w_up, w_down, g_final):
    B, S, D = x.shape
    assert (S, D) == (SEQ, D_MODEL) and mem.shape == (B, N_MEM, D_MODEL)
    gain = lambda g: g.reshape(1, D_MODEL).astype(F32)
    bf = lambda w: w.astype(BF16)
    slopes = 2.0 ** (-(jnp.arange(1, N_DIL_HEADS + 1, dtype=F32) * (8.0 / N_DIL_HEADS)))

    km, vm = _mem_kv(mem, gain(g_mem), bf(w_xk), bf(w_xv))
    o_qf = 3 * DIL_WIDTH
    o_kf, o_vf = o_qf + FOX_WIDTH, o_qf + 2 * FOX_WIDTH
    pad_heads = lambda a: jnp.pad(a, ((0, 0), (0, LANES - N_FOX_HEADS)))
    qa, ka, va, qf_t, kf, vf_t, gate = _in_proj(
        x.reshape(B * S, D), gain(g_mix), bf(w_in[:, :o_qf]), bf(w_in[:, o_kf:o_vf]),
        bf(w_in[:, o_qf:o_kf].T), bf(w_in[:, o_vf:QKV_WIDTH].T), bf(pad_heads(w_in[:, QKV_WIDTH:])))
    c_pieces = _gate_cum(gate, pad_heads(b_forget.reshape(1, N_FOX_HEADS).astype(F32)), B)
    seq = lambda a: a.reshape(B, S, a.shape[-1])
    ya = _dilated(slopes, seq(qa), seq(ka), seq(va))
    yf = _fox(qf_t, seq(kf), c_pieces, vf_t)
    x = _post_mix(x, ya, yf, km, vm, bf(w_out[:DIL_WIDTH]), bf(w_out[DIL_WIDTH:]),
                  gain(g_xattn), bf(w_xq), bf(w_xo))
    y = _mlp(x.reshape(B * S, D), gain(g_mlp), bf(w_up), bf(w_down), gain(g_final))
    return y.reshape(B, S, D)
```

```python
import functools
import math

import jax
import jax.numpy as jnp
from jax import lax
from jax.experimental import pallas as pl
from jax.experimental.pallas import tpu as pltpu

D_MODEL = 1024
SEQ = 2048
N_MEM = 256
HEAD_DIM = 64
N_DIL_HEADS = 8
N_FOX_HEADS = 8
DIL_WIDTH = N_DIL_HEADS * HEAD_DIM
FOX_WIDTH = N_FOX_HEADS * HEAD_DIM
QKV_WIDTH = 3 * DIL_WIDTH + 3 * FOX_WIDTH
DIL_CONFIGS = ((128, 1), (512, 4), (2048, 16))
BLOCK = 128
N_XATTN_HEADS = 4
XATTN_HEAD_DIM = D_MODEL // N_XATTN_HEADS
D_FF = 4 * D_MODEL
EPS = 1e-6
NEG = -1e30
LOG2E = math.log2(math.e)

LANES = 128
HEAD_PAIRS = N_FOX_HEADS // 2
ROW_TILE = 512
FOX_TILE = 256
FOX_HEADS_PER_STEP = 8
ONES_ROWS = 16
FF_CHUNK = 1024
MIX_CHUNK = 256
DIL_GROUP = 16

F32 = jnp.float32
BF16 = jnp.bfloat16

_NT = (((1,), (1,)), ((), ()))


def _dot(a, b):
    return jnp.dot(a, b, preferred_element_type=F32)


def _dot_nt(a, b):
    return lax.dot_general(a, b, _NT, preferred_element_type=F32)


def _rmsnorm(x, g):
    return x * lax.rsqrt(jnp.mean(x * x, axis=-1, keepdims=True) + EPS) * g


def _resident(shape):
    zeros = (0,) * len(shape)
    return pl.BlockSpec(shape, lambda *_: zeros, pipeline_mode=pl.Buffered(1))


def _params(semantics, vmem_mib):
    return pltpu.CompilerParams(dimension_semantics=semantics,
                                vmem_limit_bytes=vmem_mib * 1024 * 1024)


def _stack_heads(q2):
    lane = lax.broadcasted_iota(jnp.int32, (1, LANES), 1)
    zero = jnp.zeros_like(q2)
    return jnp.concatenate([jnp.where(lane < HEAD_DIM, q2, zero),
                            jnp.where(lane >= HEAD_DIM, q2, zero)], axis=0)


def _unstack_heads(o):
    rows = o.shape[0] // 2
    lane = lax.broadcasted_iota(jnp.int32, (1, LANES), 1)
    return jnp.where(lane < HEAD_DIM, o[:rows], o[rows:])


def _mem_kv_kernel(mem_ref, g_ref, wk_ref, wv_ref, k_ref, v_ref):
    m = _rmsnorm(mem_ref[0], g_ref[...]).astype(BF16)
    k_ref[0] = _dot(m, wk_ref[...]).astype(BF16)
    v_ref[0] = _dot(m, wv_ref[...]).astype(BF16)


def _mem_kv(mem, g_mem, w_xk, w_xv):
    B = mem.shape[0]
    blk = pl.BlockSpec((1, N_MEM, D_MODEL), lambda b: (b, 0, 0))
    return pl.pallas_call(
        _mem_kv_kernel,
        grid=(B,),
        in_specs=[blk, _resident((1, D_MODEL)), _resident((D_MODEL, D_MODEL)),
                  _resident((D_MODEL, D_MODEL))],
        out_specs=[blk, blk],
        out_shape=[jax.ShapeDtypeStruct((B, N_MEM, D_MODEL), BF16)] * 2,
        compiler_params=_params(("parallel",), 24),
        name="mem_kv",
    )(mem, g_mem, w_xk, w_xv)


def _in_proj_kernel(x_ref, g_ref, wa_ref, wk_ref, wqt_ref, wvt_ref, wg_ref,
                    qa, ka, va, qft, kf, vft, gate):
    h = _rmsnorm(x_ref[...], g_ref[...]).astype(BF16)
    scale = LOG2E / math.sqrt(HEAD_DIM)
    qa[...] = _dot(h, wa_ref[:, :DIL_WIDTH]) * scale
    ka[...] = _dot(h, wa_ref[:, DIL_WIDTH:2 * DIL_WIDTH])
    va[...] = _dot(h, wa_ref[:, 2 * DIL_WIDTH:])
    qft[...] = (_dot_nt(wqt_ref[...], h) * scale).astype(BF16)
    kf[...] = _dot(h, wk_ref[...]).astype(BF16)
    vft[...] = _dot_nt(wvt_ref[...], h).astype(BF16)
    gate[...] = _dot(h, wg_ref[...])


def _in_proj(x2d, g_mix, w_a, w_kf, w_qf_t, w_vf_t, w_gate):
    T = x2d.shape[0]
    row = lambda i: (i, 0)
    col = lambda i: (0, i)
    rows_blk = pl.BlockSpec((ROW_TILE, DIL_WIDTH), row)
    cols_blk = pl.BlockSpec((FOX_WIDTH, ROW_TILE), col)
    return pl.pallas_call(
        _in_proj_kernel,
        grid=(T // ROW_TILE,),
        in_specs=[pl.BlockSpec((ROW_TILE, D_MODEL), row), _resident((1, D_MODEL)),
                  _resident((D_MODEL, 3 * DIL_WIDTH)), _resident((D_MODEL, FOX_WIDTH)),
                  _resident((FOX_WIDTH, D_MODEL)), _resident((FOX_WIDTH, D_MODEL)),
                  _resident((D_MODEL, LANES))],
        out_specs=[rows_blk] * 3 + [cols_blk, rows_blk, cols_blk,
                                    pl.BlockSpec((ROW_TILE, LANES), row)],
        out_shape=[jax.ShapeDtypeStruct((T, DIL_WIDTH), F32)] * 3
        + [jax.ShapeDtypeStruct((FOX_WIDTH, T), BF16), jax.ShapeDtypeStruct((T, FOX_WIDTH), BF16),
           jax.ShapeDtypeStruct((FOX_WIDTH, T), BF16), jax.ShapeDtypeStruct((T, LANES), F32)],
        compiler_params=_params(("parallel",), 40),
        name="in_proj",
    )(x2d, g_mix, w_a, w_kf, w_qf_t, w_vf_t, w_gate)


def _split3(x):
    hi = x.astype(BF16)
    rest = x - hi.astype(F32)
    mid = rest.astype(BF16)
    lo = (rest - mid.astype(F32)).astype(BF16)
    return hi, mid, lo


def _gate_cum_kernel(gate_ref, b_ref, c_ref):
    r = lax.broadcasted_iota(jnp.int32, (BLOCK, BLOCK), 0)
    c = lax.broadcasted_iota(jnp.int32, (BLOCK, BLOCK), 1)
    lower = (r >= c).astype(BF16)
    is_head = lax.broadcasted_iota(jnp.int32, (1, LANES), 1) < N_FOX_HEADS
    carry = jnp.zeros((1, LANES), F32)
    for blk in range(SEQ // BLOCK):
        rows = slice(blk * BLOCK, (blk + 1) * BLOCK)
        z = gate_ref[rows, :] + b_ref[...]
        log_f = jnp.minimum(z, 0.0) - jnp.log1p(jnp.exp(-jnp.abs(z)))
        cum = sum(_dot(lower, piece) for piece in _split3(log_f)) + carry
        carry = cum[BLOCK - 1:BLOCK, :]
        packed = sum(pltpu.roll(jnp.where(is_head, piece.astype(F32), 0.0), N_FOX_HEADS * i, axis=1)
                     if i else jnp.where(is_head, piece.astype(F32), 0.0)
                     for i, piece in enumerate(_split3(cum * LOG2E)))
        c_ref[0, rows, :] = packed.astype(BF16)


def _gate_cum(gate, b_forget, B):
    return pl.pallas_call(
        _gate_cum_kernel,
        grid=(B,),
        in_specs=[pl.BlockSpec((SEQ, LANES), lambda b: (b, 0)), _resident((1, LANES))],
        out_specs=pl.BlockSpec((1, SEQ, LANES), lambda b: (b, 0, 0)),
        out_shape=jax.ShapeDtypeStruct((B, SEQ, LANES), BF16),
        compiler_params=_params(("parallel",), 16),
        name="gate_cum",
    )(gate, b_forget)


def _dilated_kernel(slopes_ref, q_ref, k_ref, v_ref, o_ref, ob_ref, lb_ref):
    pair = pl.program_id(1)
    row = lax.broadcasted_iota(jnp.int32, (2 * BLOCK, 1), 0)
    slope = jnp.where(row < BLOCK, slopes_ref[2 * pair], slopes_ref[2 * pair + 1])
    qi = jnp.where(row < BLOCK, row, row - BLOCK)
    lane = lax.broadcasted_iota(jnp.int32, (1, LANES), 1)

    def rows(start, size, dilation):
        if dilation == 1:
            return pl.ds(start, size)
        return pl.ds(start, size, stride=dilation)

    for branch, (window, dilation) in enumerate(DIL_CONFIGS):
        steps = window // dilation
        length = SEQ // dilation
        n_blocks = length // BLOCK

        def bias_table(n_keys, first_key):
            kj = lax.broadcasted_iota(jnp.int32, (1, n_keys), 1) + first_key
            delta = qi - kj
            valid = (delta >= 0) & (delta <= steps)
            dist = (delta * dilation).astype(F32)
            return jnp.where(valid, (-LOG2E) * slope * dist, NEG)

        bias_first = bias_table(BLOCK, 0)
        bias_rest = bias_table(2 * BLOCK, -BLOCK)

        def attend(q_start, k_start, n_keys, bias):
            q2 = q_ref[0, rows(q_start, BLOCK, dilation), :].astype(BF16)
            k2 = k_ref[0, rows(k_start, n_keys, dilation), :].astype(BF16)
            v2 = v_ref[0, rows(k_start, n_keys, dilation), :].astype(BF16)
            s = _dot_nt(_stack_heads(q2), k2) + bias
            m = jnp.max(s, axis=-1, keepdims=True)
            p = jnp.exp2(s - m)
            l = jnp.sum(p, axis=-1, keepdims=True)
            o = _dot(p.astype(BF16), v2) / l
            lse = jnp.broadcast_to(m + jnp.log(l) * LOG2E, (2 * BLOCK, LANES))
            ob_ref[branch, rows(q_start, BLOCK, dilation), :] = _unstack_heads(o)
            lb_ref[branch, rows(q_start, BLOCK, dilation), :] = _unstack_heads(lse)

        def first_block(r):
            attend(r, r, BLOCK, bias_first)

        def later_block(r, n):
            attend(r + n * (dilation * BLOCK), r + (n - 1) * (dilation * BLOCK),
                   2 * BLOCK, bias_rest)

        if n_blocks == 1:
            def residue_group(g, carry):
                for u in range(DIL_GROUP):
                    first_block(g * DIL_GROUP + u)
                return carry

            lax.fori_loop(0, dilation // DIL_GROUP, residue_group, 0)
        elif n_blocks <= DIL_GROUP:
            per_body = DIL_GROUP // n_blocks

            def whole_residues(g, carry):
                for u in range(per_body):
                    first_block(g * per_body + u)
                    for n in range(1, n_blocks):
                        later_block(g * per_body + u, n)
                return carry

            lax.fori_loop(0, dilation // per_body, whole_residues, 0)
        else:
            group = max(g for g in range(1, DIL_GROUP + 1) if (n_blocks - 1) % g == 0)
            for r in range(dilation):
                first_block(r)

                def block_group(i, carry, r=r):
                    for u in range(group):
                        later_block(r, 1 + i * group + u)
                    return carry

                lax.fori_loop(0, (n_blocks - 1) // group, block_group, 0)

    def mix(i, carry):
        chunk = pl.ds(pl.multiple_of(i * MIX_CHUNK, MIX_CHUNK), MIX_CHUNK)
        lses = [lb_ref[b, chunk, :] for b in range(len(DIL_CONFIGS))]
        top = functools.reduce(jnp.maximum, lses)
        ws = [jnp.exp2(l - top) for l in lses]
        num = sum(w * ob_ref[b, chunk, :] for b, w in enumerate(ws))
        o_ref[0, chunk, :] = (num / sum(ws)).astype(o_ref.dtype)
        return carry

    lax.fori_loop(0, SEQ // MIX_CHUNK, mix, 0)


def _dilated(slopes, qa, ka, va):
    B = qa.shape[0]
    blk = pl.BlockSpec((1, SEQ, LANES), lambda b, p: (b, 0, p))
    n_br = len(DIL_CONFIGS)
    return pl.pallas_call(
        _dilated_kernel,
        grid=(B, HEAD_PAIRS),
        in_specs=[pl.BlockSpec(memory_space=pltpu.SMEM), blk, blk, blk],
        out_specs=blk,
        out_shape=jax.ShapeDtypeStruct((B, SEQ, DIL_WIDTH), BF16),
        scratch_shapes=[pltpu.VMEM((n_br, SEQ, LANES), F32),
                        pltpu.VMEM((n_br, SEQ, LANES), F32)],
        compiler_params=_params(("parallel", "parallel"), 32),
        name="dilated",
    )(slopes, qa, ka, va)


def _fox_kernel(qt_ref, k_ref, c_ref, vt_ref, o_ref,
                s_ref, p_ref, m_ref, alpha_ref, acc_ref):
    first_head = pl.program_id(1) * FOX_HEADS_PER_STEP
    heads = range(FOX_HEADS_PER_STEP)
    tile = FOX_TILE
    kpos = lax.broadcasted_iota(jnp.int32, (tile, tile), 0)
    qpos = lax.broadcasted_iota(jnp.int32, (tile, tile), 1)
    causal = kpos <= qpos
    chan = lax.broadcasted_iota(jnp.int32, (LANES, tile), 0)

    def minus_c(head):
        hit = functools.reduce(jnp.logical_or,
                               [chan == head + N_FOX_HEADS * i for i in range(3)])
        return jnp.where(hit, -1.0, 0.0).astype(BF16)

    pick_c = [minus_c(first_head + h) for h in heads]
    ones_rows = jnp.ones((ONES_ROWS, tile), BF16)
    own = [chan < HEAD_DIM, chan >= HEAD_DIM]

    def tile_slice(t):
        return pl.ds(pl.multiple_of(t * tile, tile), tile)

    def query_columns(i):
        rhs = []
        for h in heads:
            qt = qt_ref[(h // 2) * LANES:(h // 2 + 1) * LANES, tile_slice(i)]
            rhs.append(jnp.concatenate(
                [jnp.where(own[h % 2], qt, jnp.zeros_like(qt)), pick_c[h]], axis=0))
        return rhs

    def store_scores(rhs, j):
        c_k = c_ref[0, tile_slice(j), :]
        for h in heads:
            k2 = k_ref[0, tile_slice(j), (h // 2) * LANES:(h // 2 + 1) * LANES]
            s_ref[h] = _dot(jnp.concatenate([k2, c_k], axis=1), rhs[h])

    def accumulate(j):
        for h in heads:
            vt = vt_ref[h * HEAD_DIM:(h + 1) * HEAD_DIM, tile_slice(j)]
            vt_ones = jnp.concatenate([vt, ones_rows], axis=0)
            acc_ref[h] = alpha_ref[h] * acc_ref[h] + _dot(vt_ones, p_ref[h])

    def softmax_step(diagonal):
        for h in heads:
            s = jnp.where(causal, s_ref[h], NEG) if diagonal else s_ref[h]
            m = m_ref[h]
            m_new = jnp.maximum(m, jnp.max(s, axis=0, keepdims=True))
            m_ref[h] = m_new
            alpha_ref[h] = jnp.exp2(m - m_new)
            p_ref[h] = jnp.exp2(s - m_new).astype(BF16)

    def new_query_tile():
        for h in heads:
            m_ref[h] = jnp.full((1, tile), NEG, F32)

    def finish(i):
        accumulate(i)
        o_t = jnp.concatenate(
            [acc_ref[h, :HEAD_DIM, :] / acc_ref[h, HEAD_DIM:HEAD_DIM + 1, :] for h in heads],
            axis=0)
        o_ref[0, tile_slice(i), :] = o_t.T.astype(o_ref.dtype)

    n_tiles = SEQ // tile
    for h in heads:
        acc_ref[h] = jnp.zeros((HEAD_DIM + ONES_ROWS, tile), F32)
    store_scores(query_columns(0), 0)
    new_query_tile()
    softmax_step(diagonal=True)
    store_scores(query_columns(1), 0)

    def per_query_tile(i, carry):
        rhs = query_columns(i)
        finish(i - 1)
        new_query_tile()
        softmax_step(diagonal=False)
        store_scores(rhs, 1)

        def full_tile(j, carry):
            accumulate(j - 1)
            softmax_step(diagonal=False)
            store_scores(rhs, j + 1)
            return carry

        lax.fori_loop(1, i, full_tile, 0)
        accumulate(i - 1)
        softmax_step(diagonal=True)
        store_scores(query_columns(jnp.minimum(i + 1, n_tiles - 1)), 0)
        return carry

    lax.fori_loop(1, n_tiles, per_query_tile, 0)
    finish(n_tiles - 1)


def _fox(qf_t, kf, c_pieces, vf_t):
    B = kf.shape[0]
    width = FOX_HEADS_PER_STEP * HEAD_DIM
    rows_blk = pl.BlockSpec((1, SEQ, width), lambda b, g: (b, 0, g))
    cols_blk = pl.BlockSpec((width, SEQ), lambda b, g: (g, b))
    return pl.pallas_call(
        _fox_kernel,
        grid=(B, N_FOX_HEADS // FOX_HEADS_PER_STEP),
        in_specs=[cols_blk, rows_blk, pl.BlockSpec((1, SEQ, LANES), lambda b, g: (b, 0, 0)),
                  cols_blk],
        out_specs=rows_blk,
        out_shape=jax.ShapeDtypeStruct((B, SEQ, FOX_WIDTH), BF16),
        scratch_shapes=[pltpu.VMEM((FOX_HEADS_PER_STEP, FOX_TILE, FOX_TILE), F32),
                        pltpu.VMEM((FOX_HEADS_PER_STEP, FOX_TILE, FOX_TILE), BF16),
                        pltpu.VMEM((FOX_HEADS_PER_STEP, 1, FOX_TILE), F32),
                        pltpu.VMEM((FOX_HEADS_PER_STEP, 1, FOX_TILE), F32),
                        pltpu.VMEM((FOX_HEADS_PER_STEP, HEAD_DIM + ONES_ROWS, FOX_TILE), F32)],
        compiler_params=_params(("parallel", "parallel"), 32),
        name="fox",
    )(qf_t, kf, c_pieces, vf_t)


def _post_mix_kernel(x_ref, ya_ref, yf_ref, km_ref, vm_ref, woa_ref, wob_ref, g_ref,
                     wq_ref, wo_ref, o_ref):
    x = x_ref[0] + _dot(ya_ref[0], woa_ref[...]) + _dot(yf_ref[0], wob_ref[...])
    h = _rmsnorm(x, g_ref[...]).astype(BF16)
    scale = LOG2E / math.sqrt(XATTN_HEAD_DIM)
    q = (_dot(h, wq_ref[...]) * scale).astype(BF16)
    heads = []
    for hd in range(N_XATTN_HEADS):
        cols = slice(hd * XATTN_HEAD_DIM, (hd + 1) * XATTN_HEAD_DIM)
        s = _dot_nt(q[:, cols], km_ref[0, :, cols])
        p = jnp.exp2(s - jnp.max(s, axis=-1, keepdims=True))
        l = jnp.sum(p, axis=-1, keepdims=True)
        heads.append((_dot(p.astype(BF16), vm_ref[0, :, cols]) / l).astype(BF16))
    o_ref[0] = x + _dot(jnp.concatenate(heads, axis=-1), wo_ref[...])


def _post_mix(x, ya, yf, km, vm, w_oa, w_ob, g_xattn, w_xq, w_xo):
    B = x.shape[0]
    tok = lambda b, i: (b, i, 0)
    per_b = lambda b, i: (b, 0, 0)
    return pl.pallas_call(
        _post_mix_kernel,
        grid=(B, SEQ // ROW_TILE),
        in_specs=[pl.BlockSpec((1, ROW_TILE, D_MODEL), tok),
                  pl.BlockSpec((1, ROW_TILE, DIL_WIDTH), tok),
                  pl.BlockSpec((1, ROW_TILE, FOX_WIDTH), tok),
                  pl.BlockSpec((1, N_MEM, D_MODEL), per_b),
                  pl.BlockSpec((1, N_MEM, D_MODEL), per_b),
                  _resident((DIL_WIDTH, D_MODEL)), _resident((FOX_WIDTH, D_MODEL)),
                  _resident((1, D_MODEL)), _resident((D_MODEL, D_MODEL)),
                  _resident((D_MODEL, D_MODEL))],
        out_specs=pl.BlockSpec((1, ROW_TILE, D_MODEL), tok),
        out_shape=jax.ShapeDtypeStruct(x.shape, F32),
        compiler_params=_params(("parallel", "parallel"), 40),
        name="post_mix",
    )(x, ya, yf, km, vm, w_oa, w_ob, g_xattn, w_xq, w_xo)


def _mlp_kernel(x_ref, g_ref, wu_ref, wd_ref, gf_ref, o_ref):
    x = x_ref[...]
    h = _rmsnorm(x, g_ref[...]).astype(BF16)
    y = x
    for c in range(D_FF // FF_CHUNK):
        cols = slice(c * FF_CHUNK, (c + 1) * FF_CHUNK)
        a = jnp.maximum(_dot(h, wu_ref[:, cols]), 0.0)
        y = y + _dot((a * a).astype(BF16), wd_ref[cols, :])
    o_ref[...] = _rmsnorm(y, gf_ref[...])


def _mlp(x2d, g_mlp, w_up, w_down, g_final):
    T = x2d.shape[0]
    row = pl.BlockSpec((ROW_TILE, D_MODEL), lambda i: (i, 0))
    return pl.pallas_call(
        _mlp_kernel,
        grid=(T // ROW_TILE,),
        in_specs=[row, _resident((1, D_MODEL)), _resident((D_MODEL, D_FF)),
                  _resident((D_FF, D_MODEL)), _resident((1, D_MODEL))],
        out_specs=row,
        out_shape=jax.ShapeDtypeStruct(x2d.shape, F32),
        compiler_params=_params(("parallel",), 48),
        name="mlp",
    )(x2d, g_mlp, w_up, w_down, g_final)


def kernel(x, mem, g_mix, w_in, b_forget, w_out, g_xattn, g_mem, w_xq, w_xk, w_xv, w_xo,
           g_mlp, w_up, w_down, g_final):
    B, S, D = x.shape
    assert (S, D) == (SEQ, D_MODEL) and mem.shape == (B, N_MEM, D_MODEL)
    gain = lambda g: g.reshape(1, D_MODEL).astype(F32)
    bf = lambda w: w.astype(BF16)
    slopes = 2.0 ** (-(jnp.arange(1, N_DIL_HEADS + 1, dtype=F32) * (8.0 / N_DIL_HEADS)))

    km, vm = _mem_kv(mem, gain(g_mem), bf(w_xk), bf(w_xv))
    o_qf = 3 * DIL_WIDTH
    o_kf, o_vf = o_qf + FOX_WIDTH, o_qf + 2 * FOX_WIDTH
    pad_heads = lambda a: jnp.pad(a, ((0, 0), (0, LANES - N_FOX_HEADS)))
    qa, ka, va, qf_t, kf, vf_t, gate = _in_proj(
        x.reshape(B * S, D), gain(g_mix), bf(w_in[:, :o_qf]), bf(w_in[:, o_kf:o_vf]),
        bf(w_in[:, o_qf:o_kf].T), bf(w_in[:, o_vf:QKV_WIDTH].T), bf(pad_heads(w_in[:, QKV_WIDTH:])))
    c_pieces = _gate_cum(gate, pad_heads(b_forget.reshape(1, N_FOX_HEADS).astype(F32)), B)
    seq = lambda a: a.reshape(B, S, a.shape[-1])
    ya = _dilated(slopes, seq(qa), seq(ka), seq(va))
    yf = _fox(qf_t, seq(kf), c_pieces, vf_t)
    x = _post_mix(x, ya, yf, km, vm, bf(w_out[:DIL_WIDTH]), bf(w_out[DIL_WIDTH:]),
                  gain(g_xattn), bf(w_xq), bf(w_xo))
    y = _mlp(x.reshape(B * S, D), gain(g_mlp), bf(w_up), bf(w_down), gain(g_final))
    return y.reshape(B, S, D)
```

```python
import functools
import math

import jax
import jax.numpy as jnp
from jax import lax
from jax.experimental import pallas as pl
from jax.experimental.pallas import tpu as pltpu

D_MODEL = 1024
SEQ = 2048
N_MEM = 256
HEAD_DIM = 64
N_DIL_HEADS = 8
N_FOX_HEADS = 8
DIL_WIDTH = N_DIL_HEADS * HEAD_DIM
FOX_WIDTH = N_FOX_HEADS * HEAD_DIM
QKV_WIDTH = 3 * DIL_WIDTH + 3 * FOX_WIDTH
DIL_CONFIGS = ((128, 1), (512, 4), (2048, 16))
BLOCK = 128
N_XATTN_HEADS = 4
XATTN_HEAD_DIM = D_MODEL // N_XATTN_HEADS
D_FF = 4 * D_MODEL
EPS = 1e-6
NEG = -1e30
LOG2E = math.log2(math.e)

LANES = 128
HEAD_PAIRS = N_FOX_HEADS // 2
ROW_TILE = 512
FOX_TILE = 256
FOX_HEADS_PER_STEP = 8
ONES_ROWS = 16
FF_CHUNK = 1024
MIX_CHUNK = 256
DIL_GROUP = 16

F32 = jnp.float32
BF16 = jnp.bfloat16

_NT = (((1,), (1,)), ((), ()))


def _dot(a, b):
    return jnp.dot(a, b, preferred_element_type=F32)


def _dot_nt(a, b):
    return lax.dot_general(a, b, _NT, preferred_element_type=F32)


def _rmsnorm(x, g):
    return x * lax.rsqrt(jnp.mean(x * x, axis=-1, keepdims=True) + EPS) * g


def _resident(shape):
    zeros = (0,) * len(shape)
    return pl.BlockSpec(shape, lambda *_: zeros, pipeline_mode=pl.Buffered(1))


def _params(semantics, vmem_mib):
    return pltpu.CompilerParams(dimension_semantics=semantics,
                                vmem_limit_bytes=vmem_mib * 1024 * 1024)


def _stack_heads(q2):
    lane = lax.broadcasted_iota(jnp.int32, (1, LANES), 1)
    zero = jnp.zeros_like(q2)
    return jnp.concatenate([jnp.where(lane < HEAD_DIM, q2, zero),
                            jnp.where(lane >= HEAD_DIM, q2, zero)], axis=0)


def _unstack_heads(o):
    rows = o.shape[0] // 2
    lane = lax.broadcasted_iota(jnp.int32, (1, LANES), 1)
    return jnp.where(lane < HEAD_DIM, o[:rows], o[rows:])


def _mem_kv_kernel(mem_ref, g_ref, wk_ref, wv_ref, k_ref, v_ref):
    m = _rmsnorm(mem_ref[0], g_ref[...]).astype(BF16)
    k_ref[0] = _dot(m, wk_ref[...]).astype(BF16)
    v_ref[0] = _dot(m, wv_ref[...]).astype(BF16)


def _mem_kv(mem, g_mem, w_xk, w_xv):
    B = mem.shape[0]
    blk = pl.BlockSpec((1, N_MEM, D_MODEL), lambda b: (b, 0, 0))
    return pl.pallas_call(
        _mem_kv_kernel,
        grid=(B,),
        in_specs=[blk, _resident((1, D_MODEL)), _resident((D_MODEL, D_MODEL)),
                  _resident((D_MODEL, D_MODEL))],
        out_specs=[blk, blk],
        out_shape=[jax.ShapeDtypeStruct((B, N_MEM, D_MODEL), BF16)] * 2,
        compiler_params=_params(("parallel",), 24),
        name="mem_kv",
    )(mem, g_mem, w_xk, w_xv)


def _in_proj_kernel(x_ref, g_ref, wa_ref, wk_ref, wqt_ref, wvt_ref, wg_ref,
                    qa, ka, va, qft, kf, vft, gate):
    h = _rmsnorm(x_ref[...], g_ref[...]).astype(BF16)
    scale = LOG2E / math.sqrt(HEAD_DIM)
    qa[...] = _dot(h, wa_ref[:, :DIL_WIDTH]) * scale
    ka[...] = _dot(h, wa_ref[:, DIL_WIDTH:2 * DIL_WIDTH])
    va[...] = _dot(h, wa_ref[:, 2 * DIL_WIDTH:])
    qft[...] = (_dot_nt(wqt_ref[...], h) * scale).astype(BF16)
    kf[...] = _dot(h, wk_ref[...]).astype(BF16)
    vft[...] = _dot_nt(wvt_ref[...], h).astype(BF16)
    gate[...] = _dot(h, wg_ref[...])


def _in_proj(x2d, g_mix, w_a, w_kf, w_qf_t, w_vf_t, w_gate):
    T = x2d.shape[0]
    row = lambda i: (i, 0)
    col = lambda i: (0, i)
    rows_blk = pl.BlockSpec((ROW_TILE, DIL_WIDTH), row)
    cols_blk = pl.BlockSpec((FOX_WIDTH, ROW_TILE), col)
    return pl.pallas_call(
        _in_proj_kernel,
        grid=(T // ROW_TILE,),
        in_specs=[pl.BlockSpec((ROW_TILE, D_MODEL), row), _resident((1, D_MODEL)),
                  _resident((D_MODEL, 3 * DIL_WIDTH)), _resident((D_MODEL, FOX_WIDTH)),
                  _resident((FOX_WIDTH, D_MODEL)), _resident((FOX_WIDTH, D_MODEL)),
                  _resident((D_MODEL, LANES))],
        out_specs=[rows_blk] * 3 + [cols_blk, rows_blk, cols_blk,
                                    pl.BlockSpec((ROW_TILE, LANES), row)],
        out_shape=[jax.ShapeDtypeStruct((T, DIL_WIDTH), F32)] * 3
        + [jax.ShapeDtypeStruct((FOX_WIDTH, T), BF16), jax.ShapeDtypeStruct((T, FOX_WIDTH), BF16),
           jax.ShapeDtypeStruct((FOX_WIDTH, T), BF16), jax.ShapeDtypeStruct((T, LANES), F32)],
        compiler_params=_params(("parallel",), 40),
        name="in_proj",
    )(x2d, g_mix, w_a, w_kf, w_qf_t, w_vf_t, w_gate)


def _split3(x):
    hi = x.astype(BF16)
    rest = x - hi.astype(F32)
    mid = rest.astype(BF16)
    lo = (rest - mid.astype(F32)).astype(BF16)
    return hi, mid, lo


def _gate_cum_kernel(gate_ref, b_ref, c_ref):
    r = lax.broadcasted_iota(jnp.int32, (BLOCK, BLOCK), 0)
    c = lax.broadcasted_iota(jnp.int32, (BLOCK, BLOCK), 1)
    lower = (r >= c).astype(BF16)
    is_head = lax.broadcasted_iota(jnp.int32, (1, LANES), 1) < N_FOX_HEADS
    carry = jnp.zeros((1, LANES), F32)
    for blk in range(SEQ // BLOCK):
        rows = slice(blk * BLOCK, (blk + 1) * BLOCK)
        z = gate_ref[rows, :] + b_ref[...]
        log_f = jnp.minimum(z, 0.0) - jnp.log1p(jnp.exp(-jnp.abs(z)))
        cum = sum(_dot(lower, piece) for piece in _split3(log_f)) + carry
        carry = cum[BLOCK - 1:BLOCK, :]
        packed = sum(pltpu.roll(jnp.where(is_head, piece.astype(F32), 0.0), N_FOX_HEADS * i, axis=1)
                     if i else jnp.where(is_head, piece.astype(F32), 0.0)
                     for i, piece in enumerate(_split3(cum * LOG2E)))
        c_ref[0, rows, :] = packed.astype(BF16)


def _gate_cum(gate, b_forget, B):
    return pl.pallas_call(
        _gate_cum_kernel,
        grid=(B,),
        in_specs=[pl.BlockSpec((SEQ, LANES), lambda b: (b, 0)), _resident((1, LANES))],
        out_specs=pl.BlockSpec((1, SEQ, LANES), lambda b: (b, 0, 0)),
        out_shape=jax.ShapeDtypeStruct((B, SEQ, LANES), BF16),
        compiler_params=_params(("parallel",), 16),
        name="gate_cum",
    )(gate, b_forget)


def _dilated_kernel(slopes_ref, q_ref, k_ref, v_ref, o_ref, ob_ref, lb_ref):
    pair = pl.program_id(1)
    row = lax.broadcasted_iota(jnp.int32, (2 * BLOCK, 1), 0)
    slope = jnp.where(row < BLOCK, slopes_ref[2 * pair], slopes_ref[2 * pair + 1])
    qi = jnp.where(row < BLOCK, row, row - BLOCK)
    lane = lax.broadcasted_iota(jnp.int32, (1, LANES), 1)

    def rows(start, size, dilation):
        if dilation == 1:
            return pl.ds(start, size)
        return pl.ds(start, size, stride=dilation)

    for branch, (window, dilation) in enumerate(DIL_CONFIGS):
        steps = window // dilation
        length = SEQ // dilation
        n_blocks = length // BLOCK

        def bias_table(n_keys, first_key):
            kj = lax.broadcasted_iota(jnp.int32, (1, n_keys), 1) + first_key
            delta = qi - kj
            valid = (delta >= 0) & (delta <= steps)
            dist = (delta * dilation).astype(F32)
            return jnp.where(valid, (-LOG2E) * slope * dist, NEG)

        bias_first = bias_table(BLOCK, 0)
        bias_rest = bias_table(2 * BLOCK, -BLOCK)

        def attend(q_start, k_start, n_keys, bias):
            q2 = q_ref[0, rows(q_start, BLOCK, dilation), :].astype(BF16)
            k2 = k_ref[0, rows(k_start, n_keys, dilation), :].astype(BF16)
            v2 = v_ref[0, rows(k_start, n_keys, dilation), :].astype(BF16)
            s = _dot_nt(_stack_heads(q2), k2) + bias
            m = jnp.max(s, axis=-1, keepdims=True)
            p = jnp.exp2(s - m)
            l = jnp.sum(p, axis=-1, keepdims=True)
            o = _dot(p.astype(BF16), v2) / l
            lse = jnp.broadcast_to(m + jnp.log(l) * LOG2E, (2 * BLOCK, LANES))
            ob_ref[branch, rows(q_start, BLOCK, dilation), :] = _unstack_heads(o)
            lb_ref[branch, rows(q_start, BLOCK, dilation), :] = _unstack_heads(lse)

        def first_block(r):
            attend(r, r, BLOCK, bias_first)

        def later_block(r, n):
            attend(r + n * (dilation * BLOCK), r + (n - 1) * (dilation * BLOCK),
                   2 * BLOCK, bias_rest)

        if n_blocks == 1:
            def residue_group(g, carry):
                for u in range(DIL_GROUP):
                    first_block(g * DIL_GROUP + u)
                return carry

            lax.fori_loop(0, dilation // DIL_GROUP, residue_group, 0)
        elif n_blocks <= DIL_GROUP:
            per_body = DIL_GROUP // n_blocks

            def whole_residues(g, carry):
                for u in range(per_body):
                    first_block(g * per_body + u)
                    for n in range(1, n_blocks):
                        later_block(g * per_body + u, n)
                return carry

            lax.fori_loop(0, dilation // per_body, whole_residues, 0)
        else:
            group = max(g for g in range(1, DIL_GROUP + 1) if (n_blocks - 1) % g == 0)
            for r in range(dilation):
                first_block(r)

                def block_group(i, carry, r=r):
                    for u in range(group):
                        later_block(r, 1 + i * group + u)
                    return carry

                lax.fori_loop(0, (n_blocks - 1) // group, block_group, 0)

    def mix(i, carry):
        chunk = pl.ds(pl.multiple_of(i * MIX_CHUNK, MIX_CHUNK), MIX_CHUNK)
        lses = [lb_ref[b, chunk, :] for b in range(len(DIL_CONFIGS))]
        top = functools.reduce(jnp.maximum, lses)
        ws = [jnp.exp2(l - top) for l in lses]
        num = sum(w * ob_ref[b, chunk, :] for b, w in enumerate(ws))
        o_ref[0, chunk, :] = (num / sum(ws)).astype(o_ref.dtype)
        return carry

    lax.fori_loop(0, SEQ // MIX_CHUNK, mix, 0)


def _dilated(slopes, qa, ka, va):
    B = qa.shape[0]
    blk = pl.BlockSpec((1, SEQ, LANES), lambda b, p: (b, 0, p))
    n_br = len(DIL_CONFIGS)
    return pl.pallas_call(
        _dilated_kernel,
        grid=(B, HEAD_PAIRS),
        in_specs=[pl.BlockSpec(memory_space=pltpu.SMEM), blk, blk, blk],
        out_specs=blk,
        out_shape=jax.ShapeDtypeStruct((B, SEQ, DIL_WIDTH), BF16),
        scratch_shapes=[pltpu.VMEM((n_br, SEQ, LANES), F32),
                        pltpu.VMEM((n_br, SEQ, LANES), F32)],
        compiler_params=_params(("parallel", "parallel"), 32),
        name="dilated",
    )(slopes, qa, ka, va)


def _fox_kernel(qt_ref, k_ref, c_ref, vt_ref, o_ref,
                s_ref, p_ref, m_ref, alpha_ref, acc_ref):
    first_head = pl.program_id(1) * FOX_HEADS_PER_STEP
    heads = range(FOX_HEADS_PER_STEP)
    tile = FOX_TILE
    kpos = lax.broadcasted_iota(jnp.int32, (tile, tile), 0)
    qpos = lax.broadcasted_iota(jnp.int32, (tile, tile), 1)
    causal = kpos <= qpos
    chan = lax.broadcasted_iota(jnp.int32, (LANES, tile), 0)

    def minus_c(head):
        hit = functools.reduce(jnp.logical_or,
                               [chan == head + N_FOX_HEADS * i for i in range(3)])
        return jnp.where(hit, -1.0, 0.0).astype(BF16)

    pick_c = [minus_c(first_head + h) for h in heads]
    ones_rows = jnp.ones((ONES_ROWS, tile), BF16)
    own = [chan < HEAD_DIM, chan >= HEAD_DIM]

    def tile_slice(t):
        return pl.ds(pl.multiple_of(t * tile, tile), tile)

    def query_columns(i):
        rhs = []
        for h in heads:
            qt = qt_ref[(h // 2) * LANES:(h // 2 + 1) * LANES, tile_slice(i)]
            rhs.append(jnp.concatenate(
                [jnp.where(own[h % 2], qt, jnp.zeros_like(qt)), pick_c[h]], axis=0))
        return rhs

    def store_scores(rhs, j, h):
        lhs = jnp.concatenate([k_ref[0, tile_slice(j), (h // 2) * LANES:(h // 2 + 1) * LANES],
                               c_ref[0, tile_slice(j), :]], axis=1)
        s_ref[h] = _dot(lhs, rhs[h])

    def accumulate(j, h):
        vt = vt_ref[h * HEAD_DIM:(h + 1) * HEAD_DIM, tile_slice(j)]
        vt_ones = jnp.concatenate([vt, ones_rows], axis=0)
        acc_ref[h] = alpha_ref[h] * acc_ref[h] + _dot(vt_ones, p_ref[h])

    def softmax_step(diagonal, h):
        s = jnp.where(causal, s_ref[h], NEG) if diagonal else s_ref[h]
        m = m_ref[h]
        m_new = jnp.maximum(m, jnp.max(s, axis=0, keepdims=True))
        m_ref[h] = m_new
        alpha_ref[h] = jnp.exp2(m - m_new)
        p_ref[h] = jnp.exp2(s - m_new).astype(BF16)

    def step(j_acc, diagonal, rhs, j_scores):
        for h in heads:
            if j_acc is not None:
                accumulate(j_acc, h)
            softmax_step(diagonal, h)
            store_scores(rhs, j_scores, h)

    def new_query_tile():
        for h in heads:
            m_ref[h] = jnp.full((1, tile), NEG, F32)

    def write_output(i):
        o_t = jnp.concatenate(
            [acc_ref[h, :HEAD_DIM, :] / acc_ref[h, HEAD_DIM:HEAD_DIM + 1, :] for h in heads],
            axis=0)
        o_ref[0, tile_slice(i), :] = o_t.T.astype(o_ref.dtype)

    n_tiles = SEQ // tile
    rhs0 = query_columns(0)
    for h in heads:
        acc_ref[h] = jnp.zeros((HEAD_DIM + ONES_ROWS, tile), F32)
        store_scores(rhs0, 0, h)
    new_query_tile()
    step(None, True, query_columns(1), 0)

    def per_query_tile(i, carry):
        rhs = query_columns(i)
        new_query_tile()
        step(i - 1, False, rhs, 1)
        write_output(i - 1)

        def full_tile(j, carry):
            step(j - 1, False, rhs, j + 1)
            return carry

        lax.fori_loop(1, i, full_tile, 0)
        step(i - 1, True, query_columns(jnp.minimum(i + 1, n_tiles - 1)), 0)
        return carry

    lax.fori_loop(1, n_tiles, per_query_tile, 0)
    for h in heads:
        accumulate(n_tiles - 1, h)
    write_output(n_tiles - 1)


def _fox(qf_t, kf, c_pieces, vf_t):
    B = kf.shape[0]
    width = FOX_HEADS_PER_STEP * HEAD_DIM
    rows_blk = pl.BlockSpec((1, SEQ, width), lambda b, g: (b, 0, g))
    cols_blk = pl.BlockSpec((width, SEQ), lambda b, g: (g, b))
    return pl.pallas_call(
        _fox_kernel,
        grid=(B, N_FOX_HEADS // FOX_HEADS_PER_STEP),
        in_specs=[cols_blk, rows_blk, pl.BlockSpec((1, SEQ, LANES), lambda b, g: (b, 0, 0)),
                  cols_blk],
        out_specs=rows_blk,
        out_shape=jax.ShapeDtypeStruct((B, SEQ, FOX_WIDTH), BF16),
        scratch_shapes=[pltpu.VMEM((FOX_HEADS_PER_STEP, FOX_TILE, FOX_TILE), F32),
                        pltpu.VMEM((FOX_HEADS_PER_STEP, FOX_TILE, FOX_TILE), BF16),
                        pltpu.VMEM((FOX_HEADS_PER_STEP, 1, FOX_TILE), F32),
                        pltpu.VMEM((FOX_HEADS_PER_STEP, 1, FOX_TILE), F32),
                        pltpu.VMEM((FOX_HEADS_PER_STEP, HEAD_DIM + ONES_ROWS, FOX_TILE), F32)],
        compiler_params=_params(("parallel", "parallel"), 32),
        name="fox",
    )(qf_t, kf, c_pieces, vf_t)


def _post_mix_kernel(x_ref, ya_ref, yf_ref, km_ref, vm_ref, woa_ref, wob_ref, g_ref,
                     wq_ref, wo_ref, o_ref):
    x = x_ref[0] + _dot(ya_ref[0], woa_ref[...]) + _dot(yf_ref[0], wob_ref[...])
    h = _rmsnorm(x, g_ref[...]).astype(BF16)
    scale = LOG2E / math.sqrt(XATTN_HEAD_DIM)
    q = (_dot(h, wq_ref[...]) * scale).astype(BF16)
    heads = []
    for hd in range(N_XATTN_HEADS):
        cols = slice(hd * XATTN_HEAD_DIM, (hd + 1) * XATTN_HEAD_DIM)
        s = _dot_nt(q[:, cols], km_ref[0, :, cols])
        p = jnp.exp2(s - jnp.max(s, axis=-1, keepdims=True))
        l = jnp.sum(p, axis=-1, keepdims=True)
        heads.append((_dot(p.astype(BF16), vm_ref[0, :, cols]) / l).astype(BF16))
    o_ref[0] = x + _dot(jnp.concatenate(heads, axis=-1), wo_ref[...])


def _post_mix(x, ya, yf, km, vm, w_oa, w_ob, g_xattn, w_xq, w_xo):
    B = x.shape[0]
    tok = lambda b, i: (b, i, 0)
    per_b = lambda b, i: (b, 0, 0)
    return pl.pallas_call(
        _post_mix_kernel,
        grid=(B, SEQ // ROW_TILE),
        in_specs=[pl.BlockSpec((1, ROW_TILE, D_MODEL), tok),
                  pl.BlockSpec((1, ROW_TILE, DIL_WIDTH), tok),
                  pl.BlockSpec((1, ROW_TILE, FOX_WIDTH), tok),
                  pl.BlockSpec((1, N_MEM, D_MODEL), per_b),
                  pl.BlockSpec((1, N_MEM, D_MODEL), per_b),
                  _resident((DIL_WIDTH, D_MODEL)), _resident((FOX_WIDTH, D_MODEL)),
                  _resident((1, D_MODEL)), _resident((D_MODEL, D_MODEL)),
                  _resident((D_MODEL, D_MODEL))],
        out_specs=pl.BlockSpec((1, ROW_TILE, D_MODEL), tok),
        out_shape=jax.ShapeDtypeStruct(x.shape, F32),
        compiler_params=_params(("parallel", "parallel"), 40),
        name="post_mix",
    )(x, ya, yf, km, vm, w_oa, w_ob, g_xattn, w_xq, w_xo)


def _mlp_kernel(x_ref, g_ref, wu_ref, wd_ref, gf_ref, o_ref):
    x = x_ref[...]
    h = _rmsnorm(x, g_ref[...]).astype(BF16)
    y = x
    for c in range(D_FF // FF_CHUNK):
        cols = slice(c * FF_CHUNK, (c + 1) * FF_CHUNK)
        a = jnp.maximum(_dot(h, wu_ref[:, cols]), 0.0)
        y = y + _dot((a * a).astype(BF16), wd_ref[cols, :])
    o_ref[...] = _rmsnorm(y, gf_ref[...])


def _mlp(x2d, g_mlp, w_up, w_down, g_final):
    T = x2d.shape[0]
    row = pl.BlockSpec((ROW_TILE, D_MODEL), lambda i: (i, 0))
    return pl.pallas_call(
        _mlp_kernel,
        grid=(T // ROW_TILE,),
        in_specs=[row, _resident((1, D_MODEL)), _resident((D_MODEL, D_FF)),
                  _resident((D_FF, D_MODEL)), _resident((1, D_MODEL))],
        out_specs=row,
        out_shape=jax.ShapeDtypeStruct(x2d.shape, F32),
        compiler_params=_params(("parallel",), 48),
        name="mlp",
    )(x2d, g_mlp, w_up, w_down, g_final)


def kernel(x, mem, g_mix, w_in, b_forget, w_out, g_xattn, g_mem, w_xq, w_xk, w_xv, w_xo,
           g_mlp, w_up, w_down, g_final):
    B, S, D = x.shape
    assert (S, D) == (SEQ, D_MODEL) and mem.shape == (B, N_MEM, D_MODEL)
    gain = lambda g: g.reshape(1, D_MODEL).astype(F32)
    bf = lambda w: w.astype(BF16)
    slopes = 2.0 ** (-(jnp.arange(1, N_DIL_HEADS + 1, dtype=F32) * (8.0 / N_DIL_HEADS)))

    km, vm = _mem_kv(mem, gain(g_mem), bf(w_xk), bf(w_xv))
    o_qf = 3 * DIL_WIDTH
    o_kf, o_vf = o_qf + FOX_WIDTH, o_qf + 2 * FOX_WIDTH
    pad_heads = lambda a: jnp.pad(a, ((0, 0), (0, LANES - N_FOX_HEADS)))
    qa, ka, va, qf_t, kf, vf_t, gate = _in_proj(
        x.reshape(B * S, D), gain(g_mix), bf(w_in[:, :o_qf]), bf(w_in[:, o_kf:o_vf]),
        bf(w_in[:, o_qf:o_kf].T), bf(w_in[:, o_vf:QKV_WIDTH].T), bf(pad_heads(w_in[:, QKV_WIDTH:])))
    c_pieces = _gate_cum(gate, pad_heads(b_forget.reshape(1, N_FOX_HEADS).astype(F32)), B)
    seq = lambda a: a.reshape(B, S, a.shape[-1])
    ya = _dilated(slopes, seq(qa), seq(ka), seq(va))
    yf = _fox(qf_t, seq(kf), c_pieces, vf_t)
    x = _post_mix(x, ya, yf, km, vm, bf(w_out[:DIL_WIDTH]), bf(w_out[DIL_WIDTH:]),
                  gain(g_xattn), bf(w_xq), bf(w_xo))
    y = _mlp(x.reshape(B * S, D), gain(g_mlp), bf(w_up), bf(w_down), gain(g_final))
    return y.reshape(B, S, D)
```

```python
import functools
import math

import jax
import jax.numpy as jnp
from jax import lax
from jax.experimental import pallas as pl
from jax.experimental.pallas import tpu as pltpu

D_MODEL = 1024
SEQ = 2048
N_MEM = 256
HEAD_DIM = 64
N_DIL_HEADS = 8
N_FOX_HEADS = 8
DIL_WIDTH = N_DIL_HEADS * HEAD_DIM
FOX_WIDTH = N_FOX_HEADS * HEAD_DIM
QKV_WIDTH = 3 * DIL_WIDTH + 3 * FOX_WIDTH
DIL_CONFIGS = ((128, 1), (512, 4), (2048, 16))
BLOCK = 128
N_XATTN_HEADS = 4
XATTN_HEAD_DIM = D_MODEL // N_XATTN_HEADS
D_FF = 4 * D_MODEL
EPS = 1e-6
NEG = -1e30
LOG2E = math.log2(math.e)

LANES = 128
HEAD_PAIRS = N_FOX_HEADS // 2
ROW_TILE = 512
FOX_TILE = 256
FOX_HEADS_PER_STEP = 8
ONES_ROWS = 16
FF_CHUNK = 1024
MIX_CHUNK = 256
DIL_GROUP = 16

F32 = jnp.float32
BF16 = jnp.bfloat16

_NT = (((1,), (1,)), ((), ()))


def _dot(a, b):
    return jnp.dot(a, b, preferred_element_type=F32)


def _dot_nt(a, b):
    return lax.dot_general(a, b, _NT, preferred_element_type=F32)


def _rmsnorm(x, g):
    return x * lax.rsqrt(jnp.mean(x * x, axis=-1, keepdims=True) + EPS) * g


def _resident(shape):
    zeros = (0,) * len(shape)
    return pl.BlockSpec(shape, lambda *_: zeros, pipeline_mode=pl.Buffered(1))


def _params(semantics, vmem_mib):
    return pltpu.CompilerParams(dimension_semantics=semantics,
                                vmem_limit_bytes=vmem_mib * 1024 * 1024)


def _stack_heads(q2):
    lane = lax.broadcasted_iota(jnp.int32, (1, LANES), 1)
    zero = jnp.zeros_like(q2)
    return jnp.concatenate([jnp.where(lane < HEAD_DIM, q2, zero),
                            jnp.where(lane >= HEAD_DIM, q2, zero)], axis=0)


def _unstack_heads(o):
    rows = o.shape[0] // 2
    lane = lax.broadcasted_iota(jnp.int32, (1, LANES), 1)
    return jnp.where(lane < HEAD_DIM, o[:rows], o[rows:])


def _mem_kv_kernel(mem_ref, g_ref, wk_ref, wv_ref, k_ref, v_ref):
    m = _rmsnorm(mem_ref[0], g_ref[...]).astype(BF16)
    k_ref[0] = _dot(m, wk_ref[...]).astype(BF16)
    v_ref[0] = _dot(m, wv_ref[...]).astype(BF16)


def _mem_kv(mem, g_mem, w_xk, w_xv):
    B = mem.shape[0]
    blk = pl.BlockSpec((1, N_MEM, D_MODEL), lambda b: (b, 0, 0))
    return pl.pallas_call(
        _mem_kv_kernel,
        grid=(B,),
        in_specs=[blk, _resident((1, D_MODEL)), _resident((D_MODEL, D_MODEL)),
                  _resident((D_MODEL, D_MODEL))],
        out_specs=[blk, blk],
        out_shape=[jax.ShapeDtypeStruct((B, N_MEM, D_MODEL), BF16)] * 2,
        compiler_params=_params(("parallel",), 24),
        name="mem_kv",
    )(mem, g_mem, w_xk, w_xv)


def _in_proj_kernel(x_ref, g_ref, wa_ref, wk_ref, wqt_ref, wvt_ref, wg_ref,
                    qa, ka, va, qft, kf, vft, gate):
    h = _rmsnorm(x_ref[...], g_ref[...]).astype(BF16)
    scale = LOG2E / math.sqrt(HEAD_DIM)
    qa[...] = _dot(h, wa_ref[:, :DIL_WIDTH]) * scale
    ka[...] = _dot(h, wa_ref[:, DIL_WIDTH:2 * DIL_WIDTH])
    va[...] = _dot(h, wa_ref[:, 2 * DIL_WIDTH:])
    qft[...] = (_dot_nt(wqt_ref[...], h) * scale).astype(BF16)
    kf[...] = _dot(h, wk_ref[...]).astype(BF16)
    vft[...] = _dot_nt(wvt_ref[...], h).astype(BF16)
    gate[...] = _dot(h, wg_ref[...])


def _in_proj(x2d, g_mix, w_a, w_kf, w_qf_t, w_vf_t, w_gate):
    T = x2d.shape[0]
    row = lambda i: (i, 0)
    col = lambda i: (0, i)
    rows_blk = pl.BlockSpec((ROW_TILE, DIL_WIDTH), row)
    cols_blk = pl.BlockSpec((FOX_WIDTH, ROW_TILE), col)
    return pl.pallas_call(
        _in_proj_kernel,
        grid=(T // ROW_TILE,),
        in_specs=[pl.BlockSpec((ROW_TILE, D_MODEL), row), _resident((1, D_MODEL)),
                  _resident((D_MODEL, 3 * DIL_WIDTH)), _resident((D_MODEL, FOX_WIDTH)),
                  _resident((FOX_WIDTH, D_MODEL)), _resident((FOX_WIDTH, D_MODEL)),
                  _resident((D_MODEL, LANES))],
        out_specs=[rows_blk] * 3 + [cols_blk, rows_blk, cols_blk,
                                    pl.BlockSpec((ROW_TILE, LANES), row)],
        out_shape=[jax.ShapeDtypeStruct((T, DIL_WIDTH), F32)] * 3
        + [jax.ShapeDtypeStruct((FOX_WIDTH, T), BF16), jax.ShapeDtypeStruct((T, FOX_WIDTH), BF16),
           jax.ShapeDtypeStruct((FOX_WIDTH, T), BF16), jax.ShapeDtypeStruct((T, LANES), F32)],
        compiler_params=_params(("parallel",), 40),
        name="in_proj",
    )(x2d, g_mix, w_a, w_kf, w_qf_t, w_vf_t, w_gate)


def _split3(x):
    hi = x.astype(BF16)
    rest = x - hi.astype(F32)
    mid = rest.astype(BF16)
    lo = (rest - mid.astype(F32)).astype(BF16)
    return hi, mid, lo


def _gate_cum_kernel(gate_ref, b_ref, c_ref):
    r = lax.broadcasted_iota(jnp.int32, (BLOCK, BLOCK), 0)
    c = lax.broadcasted_iota(jnp.int32, (BLOCK, BLOCK), 1)
    lower = (r >= c).astype(BF16)
    is_head = lax.broadcasted_iota(jnp.int32, (1, LANES), 1) < N_FOX_HEADS
    carry = jnp.zeros((1, LANES), F32)
    for blk in range(SEQ // BLOCK):
        rows = slice(blk * BLOCK, (blk + 1) * BLOCK)
        z = gate_ref[rows, :] + b_ref[...]
        log_f = jnp.minimum(z, 0.0) - jnp.log1p(jnp.exp(-jnp.abs(z)))
        cum = sum(_dot(lower, piece) for piece in _split3(log_f)) + carry
        carry = cum[BLOCK - 1:BLOCK, :]
        packed = sum(pltpu.roll(jnp.where(is_head, piece.astype(F32), 0.0), N_FOX_HEADS * i, axis=1)
                     if i else jnp.where(is_head, piece.astype(F32), 0.0)
                     for i, piece in enumerate(_split3(cum * LOG2E)))
        c_ref[0, rows, :] = packed.astype(BF16)


def _gate_cum(gate, b_forget, B):
    return pl.pallas_call(
        _gate_cum_kernel,
        grid=(B,),
        in_specs=[pl.BlockSpec((SEQ, LANES), lambda b: (b, 0)), _resident((1, LANES))],
        out_specs=pl.BlockSpec((1, SEQ, LANES), lambda b: (b, 0, 0)),
        out_shape=jax.ShapeDtypeStruct((B, SEQ, LANES), BF16),
        compiler_params=_params(("parallel",), 16),
        name="gate_cum",
    )(gate, b_forget)


def _dilated_kernel(slopes_ref, q_ref, k_ref, v_ref, o_ref, ob_ref, lb_ref):
    pair = pl.program_id(1)
    row = lax.broadcasted_iota(jnp.int32, (2 * BLOCK, 1), 0)
    slope = jnp.where(row < BLOCK, slopes_ref[2 * pair], slopes_ref[2 * pair + 1])
    qi = jnp.where(row < BLOCK, row, row - BLOCK)
    lane = lax.broadcasted_iota(jnp.int32, (1, LANES), 1)

    def rows(start, size, dilation):
        if dilation == 1:
            return pl.ds(start, size)
        return pl.ds(start, size, stride=dilation)

    for branch, (window, dilation) in enumerate(DIL_CONFIGS):
        steps = window // dilation
        length = SEQ // dilation
        n_blocks = length // BLOCK

        def bias_table(n_keys, first_key):
            kj = lax.broadcasted_iota(jnp.int32, (1, n_keys), 1) + first_key
            delta = qi - kj
            valid = (delta >= 0) & (delta <= steps)
            dist = (delta * dilation).astype(F32)
            return jnp.where(valid, (-LOG2E) * slope * dist, NEG)

        bias_first = bias_table(BLOCK, 0)
        bias_rest = bias_table(2 * BLOCK, -BLOCK)

        def attend(q_start, k_start, n_keys, bias):
            q2 = q_ref[0, rows(q_start, BLOCK, dilation), :].astype(BF16)
            k2 = k_ref[0, rows(k_start, n_keys, dilation), :].astype(BF16)
            v2 = v_ref[0, rows(k_start, n_keys, dilation), :].astype(BF16)
            s = _dot_nt(_stack_heads(q2), k2) + bias
            m = jnp.max(s, axis=-1, keepdims=True)
            p = jnp.exp2(s - m)
            l = jnp.sum(p, axis=-1, keepdims=True)
            o = _dot(p.astype(BF16), v2) / l
            lse = jnp.broadcast_to(m + jnp.log(l) * LOG2E, (2 * BLOCK, LANES))
            ob_ref[branch, rows(q_start, BLOCK, dilation), :] = _unstack_heads(o)
            lb_ref[branch, rows(q_start, BLOCK, dilation), :] = _unstack_heads(lse)

        def first_block(r):
            attend(r, r, BLOCK, bias_first)

        def later_block(r, n):
            attend(r + n * (dilation * BLOCK), r + (n - 1) * (dilation * BLOCK),
                   2 * BLOCK, bias_rest)

        if n_blocks == 1:
            def residue_group(g, carry):
                for u in range(DIL_GROUP):
                    first_block(g * DIL_GROUP + u)
                return carry

            lax.fori_loop(0, dilation // DIL_GROUP, residue_group, 0)
        elif n_blocks <= DIL_GROUP:
            per_body = DIL_GROUP // n_blocks

            def whole_residues(g, carry):
                for u in range(per_body):
                    first_block(g * per_body + u)
                    for n in range(1, n_blocks):
                        later_block(g * per_body + u, n)
                return carry

            lax.fori_loop(0, dilation // per_body, whole_residues, 0)
        else:
            group = max(g for g in range(1, DIL_GROUP + 1) if (n_blocks - 1) % g == 0)
            for r in range(dilation):
                first_block(r)

                def block_group(i, carry, r=r):
                    for u in range(group):
                        later_block(r, 1 + i * group + u)
                    return carry

                lax.fori_loop(0, (n_blocks - 1) // group, block_group, 0)

    def mix(i, carry):
        chunk = pl.ds(pl.multiple_of(i * MIX_CHUNK, MIX_CHUNK), MIX_CHUNK)
        lses = [lb_ref[b, chunk, :] for b in range(len(DIL_CONFIGS))]
        top = functools.reduce(jnp.maximum, lses)
        ws = [jnp.exp2(l - top) for l in lses]
        num = sum(w * ob_ref[b, chunk, :] for b, w in enumerate(ws))
        o_ref[0, chunk, :] = (num / sum(ws)).astype(o_ref.dtype)
        return carry

    lax.fori_loop(0, SEQ // MIX_CHUNK, mix, 0)


def _dilated(slopes, qa, ka, va):
    B = qa.shape[0]
    blk = pl.BlockSpec((1, SEQ, LANES), lambda b, p: (b, 0, p))
    n_br = len(DIL_CONFIGS)
    return pl.pallas_call(
        _dilated_kernel,
        grid=(B, HEAD_PAIRS),
        in_specs=[pl.BlockSpec(memory_space=pltpu.SMEM), blk, blk, blk],
        out_specs=blk,
        out_shape=jax.ShapeDtypeStruct((B, SEQ, DIL_WIDTH), BF16),
        scratch_shapes=[pltpu.VMEM((n_br, SEQ, LANES), F32),
                        pltpu.VMEM((n_br, SEQ, LANES), F32)],
        compiler_params=_params(("parallel", "parallel"), 32),
        name="dilated",
    )(slopes, qa, ka, va)


def _fox_kernel(qt_ref, k_ref, c_ref, vt_ref, o_ref,
                s_ref, p_ref, m_ref, alpha_ref, acc_ref):
    first_head = pl.program_id(1) * FOX_HEADS_PER_STEP
    heads = range(FOX_HEADS_PER_STEP)
    tile = FOX_TILE
    kpos = lax.broadcasted_iota(jnp.int32, (tile, tile), 0)
    qpos = lax.broadcasted_iota(jnp.int32, (tile, tile), 1)
    causal = kpos <= qpos
    chan = lax.broadcasted_iota(jnp.int32, (LANES, tile), 0)

    def minus_c(head):
        hit = functools.reduce(jnp.logical_or,
                               [chan == head + N_FOX_HEADS * i for i in range(3)])
        return jnp.where(hit, -1.0, 0.0).astype(BF16)

    pick_c = [minus_c(first_head + h) for h in heads]
    ones_rows = jnp.ones((ONES_ROWS, tile), BF16)
    own = [chan < HEAD_DIM, chan >= HEAD_DIM]

    def tile_slice(t):
        return pl.ds(pl.multiple_of(t * tile, tile), tile)

    def query_columns(i):
        rhs = []
        for h in heads:
            qt = qt_ref[(h // 2) * LANES:(h // 2 + 1) * LANES, tile_slice(i)]
            rhs.append(jnp.concatenate(
                [jnp.where(own[h % 2], qt, jnp.zeros_like(qt)), pick_c[h]], axis=0))
        return rhs

    def store_scores(rhs, j, h):
        lhs = jnp.concatenate([k_ref[0, tile_slice(j), (h // 2) * LANES:(h // 2 + 1) * LANES],
                               c_ref[0, tile_slice(j), :]], axis=1)
        s_ref[h] = _dot(lhs, rhs[h])

    def accumulate(j, h):
        vt = vt_ref[h * HEAD_DIM:(h + 1) * HEAD_DIM, tile_slice(j)]
        vt_ones = jnp.concatenate([vt, ones_rows], axis=0)
        acc_ref[h] = alpha_ref[h] * acc_ref[h] + _dot(vt_ones, p_ref[h])

    def softmax_step(diagonal, h):
        s = jnp.where(causal, s_ref[h], NEG) if diagonal else s_ref[h]
        m = m_ref[h]
        m_new = jnp.maximum(m, jnp.max(s, axis=0, keepdims=True))
        m_ref[h] = m_new
        alpha_ref[h] = jnp.exp2(m - m_new)
        p_ref[h] = jnp.exp2(s - m_new).astype(BF16)

    def step(j_acc, diagonal, rhs, j_scores):
        for h in heads:
            if j_acc is not None:
                accumulate(j_acc, h)
            softmax_step(diagonal, h)
            store_scores(rhs, j_scores, h)

    def new_query_tile():
        for h in heads:
            m_ref[h] = jnp.full((1, tile), NEG, F32)

    def write_output(i):
        o_t = jnp.concatenate(
            [acc_ref[h, :HEAD_DIM, :] / acc_ref[h, HEAD_DIM:HEAD_DIM + 1, :] for h in heads],
            axis=0)
        o_ref[0, tile_slice(i), :] = o_t.T.astype(o_ref.dtype)

    n_tiles = SEQ // tile
    rhs0 = query_columns(0)
    for h in heads:
        acc_ref[h] = jnp.zeros((HEAD_DIM + ONES_ROWS, tile), F32)
        store_scores(rhs0, 0, h)
    new_query_tile()
    step(None, True, query_columns(1), 0)

    def per_query_tile(i, carry):
        rhs = query_columns(i)
        new_query_tile()
        step(i - 1, False, rhs, 1)
        write_output(i - 1)

        def two_tiles(t, carry):
            j = 2 * t + 1
            step(j - 1, False, rhs, j + 1)
            step(j, False, rhs, j + 2)
            return carry

        lax.fori_loop(0, lax.shift_right_logical(i - 1, 1), two_tiles, 0)

        @pl.when((i - 1) % 2 == 1)
        def _():
            step(i - 2, False, rhs, i)
        step(i - 1, True, query_columns(jnp.minimum(i + 1, n_tiles - 1)), 0)
        return carry

    lax.fori_loop(1, n_tiles, per_query_tile, 0)
    for h in heads:
        accumulate(n_tiles - 1, h)
    write_output(n_tiles - 1)


def _fox(qf_t, kf, c_pieces, vf_t):
    B = kf.shape[0]
    width = FOX_HEADS_PER_STEP * HEAD_DIM
    rows_blk = pl.BlockSpec((1, SEQ, width), lambda b, g: (b, 0, g))
    cols_blk = pl.BlockSpec((width, SEQ), lambda b, g: (g, b))
    return pl.pallas_call(
        _fox_kernel,
        grid=(B, N_FOX_HEADS // FOX_HEADS_PER_STEP),
        in_specs=[cols_blk, rows_blk, pl.BlockSpec((1, SEQ, LANES), lambda b, g: (b, 0, 0)),
                  cols_blk],
        out_specs=rows_blk,
        out_shape=jax.ShapeDtypeStruct((B, SEQ, FOX_WIDTH), BF16),
        scratch_shapes=[pltpu.VMEM((FOX_HEADS_PER_STEP, FOX_TILE, FOX_TILE), F32),
                        pltpu.VMEM((FOX_HEADS_PER_STEP, FOX_TILE, FOX_TILE), BF16),
                        pltpu.VMEM((FOX_HEADS_PER_STEP, 1, FOX_TILE), F32),
                        pltpu.VMEM((FOX_HEADS_PER_STEP, 1, FOX_TILE), F32),
                        pltpu.VMEM((FOX_HEADS_PER_STEP, HEAD_DIM + ONES_ROWS, FOX_TILE), F32)],
        compiler_params=_params(("parallel", "parallel"), 32),
        name="fox",
    )(qf_t, kf, c_pieces, vf_t)


def _post_mix_kernel(x_ref, ya_ref, yf_ref, km_ref, vm_ref, woa_ref, wob_ref, g_ref,
                     wq_ref, wo_ref, o_ref):
    x = x_ref[0] + _dot(ya_ref[0], woa_ref[...]) + _dot(yf_ref[0], wob_ref[...])
    h = _rmsnorm(x, g_ref[...]).astype(BF16)
    scale = LOG2E / math.sqrt(XATTN_HEAD_DIM)
    q = (_dot(h, wq_ref[...]) * scale).astype(BF16)
    heads = []
    for hd in range(N_XATTN_HEADS):
        cols = slice(hd * XATTN_HEAD_DIM, (hd + 1) * XATTN_HEAD_DIM)
        s = _dot_nt(q[:, cols], km_ref[0, :, cols])
        p = jnp.exp2(s - jnp.max(s, axis=-1, keepdims=True))
        l = jnp.sum(p, axis=-1, keepdims=True)
        heads.append((_dot(p.astype(BF16), vm_ref[0, :, cols]) / l).astype(BF16))
    o_ref[0] = x + _dot(jnp.concatenate(heads, axis=-1), wo_ref[...])


def _post_mix(x, ya, yf, km, vm, w_oa, w_ob, g_xattn, w_xq, w_xo):
    B = x.shape[0]
    tok = lambda b, i: (b, i, 0)
    per_b = lambda b, i: (b, 0, 0)
    return pl.pallas_call(
        _post_mix_kernel,
        grid=(B, SEQ // ROW_TILE),
        in_specs=[pl.BlockSpec((1, ROW_TILE, D_MODEL), tok),
                  pl.BlockSpec((1, ROW_TILE, DIL_WIDTH), tok),
                  pl.BlockSpec((1, ROW_TILE, FOX_WIDTH), tok),
                  pl.BlockSpec((1, N_MEM, D_MODEL), per_b),
                  pl.BlockSpec((1, N_MEM, D_MODEL), per_b),
                  _resident((DIL_WIDTH, D_MODEL)), _resident((FOX_WIDTH, D_MODEL)),
                  _resident((1, D_MODEL)), _resident((D_MODEL, D_MODEL)),
                  _resident((D_MODEL, D_MODEL))],
        out_specs=pl.BlockSpec((1, ROW_TILE, D_MODEL), tok),
        out_shape=jax.ShapeDtypeStruct(x.shape, F32),
        compiler_params=_params(("parallel", "parallel"), 40),
        name="post_mix",
    )(x, ya, yf, km, vm, w_oa, w_ob, g_xattn, w_xq, w_xo)


def _mlp_kernel(x_ref, g_ref, wu_ref, wd_ref, gf_ref, o_ref):
    x = x_ref[...]
    h = _rmsnorm(x, g_ref[...]).astype(BF16)
    y = x
    for c in range(D_FF // FF_CHUNK):
        cols = slice(c * FF_CHUNK, (c + 1) * FF_CHUNK)
        a = jnp.maximum(_dot(h, wu_ref[:, cols]), 0.0)
        y = y + _dot((a * a).astype(BF16), wd_ref[cols, :])
    o_ref[...] = _rmsnorm(y, gf_ref[...])


def _mlp(x2d, g_mlp, w_up, w_down, g_final):
    T = x2d.shape[0]
    row = pl.BlockSpec((ROW_TILE, D_MODEL), lambda i: (i, 0))
    return pl.pallas_call(
        _mlp_kernel,
        grid=(T // ROW_TILE,),
        in_specs=[row, _resident((1, D_MODEL)), _resident((D_MODEL, D_FF)),
                  _resident((D_FF, D_MODEL)), _resident((1, D_MODEL))],
        out_specs=row,
        out_shape=jax.ShapeDtypeStruct(x2d.shape, F32),
        compiler_params=_params(("parallel",), 48),
        name="mlp",
    )(x2d, g_mlp, w_up, w_down, g_final)


def kernel(x, mem, g_mix, w_in, b_forget, w_out, g_xattn, g_mem, w_xq, w_xk, w_xv, w_xo,
           g_mlp, w_up, w_down, g_final):
    B, S, D = x.shape
    assert (S, D) == (SEQ, D_MODEL) and mem.shape == (B, N_MEM, D_MODEL)
    gain = lambda g: g.reshape(1, D_MODEL).astype(F32)
    bf = lambda w: w.astype(BF16)
    slopes = 2.0 ** (-(jnp.arange(1, N_DIL_HEADS + 1, dtype=F32) * (8.0 / N_DIL_HEADS)))

    km, vm = _mem_kv(mem, gain(g_mem), bf(w_xk), bf(w_xv))
    o_qf = 3 * DIL_WIDTH
    o_kf, o_vf = o_qf + FOX_WIDTH, o_qf + 2 * FOX_WIDTH
    pad_heads = lambda a: jnp.pad(a, ((0, 0), (0, LANES - N_FOX_HEADS)))
    qa, ka, va, qf_t, kf, vf_t, gate = _in_proj(
        x.reshape(B * S, D), gain(g_mix), bf(w_in[:, :o_qf]), bf(w_in[:, o_kf:o_vf]),
        bf(w_in[:, o_qf:o_kf].T), bf(w_in[:, o_vf:QKV_WIDTH].T), bf(pad_heads(w_in[:, QKV_WIDTH:])))
    c_pieces = _gate_cum(gate, pad_heads(b_forget.reshape(1, N_FOX_HEADS).astype(F32)), B)
    seq = lambda a: a.reshape(B, S, a.shape[-1])
    ya = _dilated(slopes, seq(qa), seq(ka), seq(va))
    yf = _fox(qf_t, seq(kf), c_pieces, vf_t)
    x = _post_mix(x, ya, yf, km, vm, bf(w_out[:DIL_WIDTH]), bf(w_out[DIL_WIDTH:]),
                  gain(g_xattn), bf(w_xq), bf(w_xo))
    y = _mlp(x.reshape(B * S, D), gain(g_mlp), bf(w_up), bf(w_down), gain(g_final))
    return y.reshape(B, S, D)
```

```python
import functools
import math

import jax
import jax.numpy as jnp
from jax import lax
from jax.experimental import pallas as pl
from jax.experimental.pallas import tpu as pltpu

D_MODEL = 1024
SEQ = 2048
N_MEM = 256
HEAD_DIM = 64
N_DIL_HEADS = 8
N_FOX_HEADS = 8
DIL_WIDTH = N_DIL_HEADS * HEAD_DIM
FOX_WIDTH = N_FOX_HEADS * HEAD_DIM
QKV_WIDTH = 3 * DIL_WIDTH + 3 * FOX_WIDTH
DIL_CONFIGS = ((128, 1), (512, 4), (2048, 16))
BLOCK = 128
N_XATTN_HEADS = 4
XATTN_HEAD_DIM = D_MODEL // N_XATTN_HEADS
D_FF = 4 * D_MODEL
EPS = 1e-6
NEG = -1e30
LOG2E = math.log2(math.e)

LANES = 128
HEAD_PAIRS = N_FOX_HEADS // 2
ROW_TILE = 512
FOX_TILE = 256
FOX_HEADS_PER_STEP = 8
ONES_ROWS = 16
FF_CHUNK = 1024
MIX_CHUNK = 256
DIL_SKEW = 1

F32 = jnp.float32
BF16 = jnp.bfloat16

_NT = (((1,), (1,)), ((), ()))


def _dot(a, b):
    return jnp.dot(a, b, preferred_element_type=F32)


def _dot_nt(a, b):
    return lax.dot_general(a, b, _NT, preferred_element_type=F32)


def _rmsnorm(x, g):
    return x * lax.rsqrt(jnp.mean(x * x, axis=-1, keepdims=True) + EPS) * g


def _resident(shape):
    zeros = (0,) * len(shape)
    return pl.BlockSpec(shape, lambda *_: zeros, pipeline_mode=pl.Buffered(1))


def _params(semantics, vmem_mib):
    return pltpu.CompilerParams(dimension_semantics=semantics,
                                vmem_limit_bytes=vmem_mib * 1024 * 1024)


def _stack_heads(q2):
    lane = lax.broadcasted_iota(jnp.int32, (1, LANES), 1)
    zero = jnp.zeros_like(q2)
    return jnp.concatenate([jnp.where(lane < HEAD_DIM, q2, zero),
                            jnp.where(lane >= HEAD_DIM, q2, zero)], axis=0)


def _unstack_heads(o):
    rows = o.shape[0] // 2
    lane = lax.broadcasted_iota(jnp.int32, (1, LANES), 1)
    return jnp.where(lane < HEAD_DIM, o[:rows], o[rows:])


def _mem_kv_kernel(mem_ref, g_ref, wk_ref, wv_ref, k_ref, v_ref):
    m = _rmsnorm(mem_ref[0], g_ref[...]).astype(BF16)
    k_ref[0] = _dot(m, wk_ref[...]).astype(BF16)
    v_ref[0] = _dot(m, wv_ref[...]).astype(BF16)


def _mem_kv(mem, g_mem, w_xk, w_xv):
    B = mem.shape[0]
    blk = pl.BlockSpec((1, N_MEM, D_MODEL), lambda b: (b, 0, 0))
    return pl.pallas_call(
        _mem_kv_kernel,
        grid=(B,),
        in_specs=[blk, _resident((1, D_MODEL)), _resident((D_MODEL, D_MODEL)),
                  _resident((D_MODEL, D_MODEL))],
        out_specs=[blk, blk],
        out_shape=[jax.ShapeDtypeStruct((B, N_MEM, D_MODEL), BF16)] * 2,
        compiler_params=_params(("parallel",), 24),
        name="mem_kv",
    )(mem, g_mem, w_xk, w_xv)


def _in_proj_kernel(x_ref, g_ref, wa_ref, wk_ref, wqt_ref, wvt_ref, wg_ref,
                    qa, ka, va, qft, kf, vft, gate):
    h = _rmsnorm(x_ref[...], g_ref[...]).astype(BF16)
    scale = LOG2E / math.sqrt(HEAD_DIM)
    qa[...] = _dot(h, wa_ref[:, :DIL_WIDTH]) * scale
    ka[...] = _dot(h, wa_ref[:, DIL_WIDTH:2 * DIL_WIDTH])
    va[...] = _dot(h, wa_ref[:, 2 * DIL_WIDTH:])
    qft[...] = (_dot_nt(wqt_ref[...], h) * scale).astype(BF16)
    kf[...] = _dot(h, wk_ref[...]).astype(BF16)
    vft[...] = _dot_nt(wvt_ref[...], h).astype(BF16)
    gate[...] = _dot(h, wg_ref[...])


def _in_proj(x2d, g_mix, w_a, w_kf, w_qf_t, w_vf_t, w_gate):
    T = x2d.shape[0]
    row = lambda i: (i, 0)
    col = lambda i: (0, i)
    rows_blk = pl.BlockSpec((ROW_TILE, DIL_WIDTH), row)
    cols_blk = pl.BlockSpec((FOX_WIDTH, ROW_TILE), col)
    return pl.pallas_call(
        _in_proj_kernel,
        grid=(T // ROW_TILE,),
        in_specs=[pl.BlockSpec((ROW_TILE, D_MODEL), row), _resident((1, D_MODEL)),
                  _resident((D_MODEL, 3 * DIL_WIDTH)), _resident((D_MODEL, FOX_WIDTH)),
                  _resident((FOX_WIDTH, D_MODEL)), _resident((FOX_WIDTH, D_MODEL)),
                  _resident((D_MODEL, LANES))],
        out_specs=[rows_blk] * 3 + [cols_blk, rows_blk, cols_blk,
                                    pl.BlockSpec((ROW_TILE, LANES), row)],
        out_shape=[jax.ShapeDtypeStruct((T, DIL_WIDTH), F32)] * 3
        + [jax.ShapeDtypeStruct((FOX_WIDTH, T), BF16), jax.ShapeDtypeStruct((T, FOX_WIDTH), BF16),
           jax.ShapeDtypeStruct((FOX_WIDTH, T), BF16), jax.ShapeDtypeStruct((T, LANES), F32)],
        compiler_params=_params(("parallel",), 40),
        name="in_proj",
    )(x2d, g_mix, w_a, w_kf, w_qf_t, w_vf_t, w_gate)


def _split3(x):
    hi = x.astype(BF16)
    rest = x - hi.astype(F32)
    mid = rest.astype(BF16)
    lo = (rest - mid.astype(F32)).astype(BF16)
    return hi, mid, lo


def _gate_cum_kernel(gate_ref, b_ref, c_ref):
    r = lax.broadcasted_iota(jnp.int32, (BLOCK, BLOCK), 0)
    c = lax.broadcasted_iota(jnp.int32, (BLOCK, BLOCK), 1)
    lower = (r >= c).astype(BF16)
    is_head = lax.broadcasted_iota(jnp.int32, (1, LANES), 1) < N_FOX_HEADS
    carry = jnp.zeros((1, LANES), F32)
    for blk in range(SEQ // BLOCK):
        rows = slice(blk * BLOCK, (blk + 1) * BLOCK)
        z = gate_ref[rows, :] + b_ref[...]
        log_f = jnp.minimum(z, 0.0) - jnp.log1p(jnp.exp(-jnp.abs(z)))
        cum = sum(_dot(lower, piece) for piece in _split3(log_f)) + carry
        carry = cum[BLOCK - 1:BLOCK, :]
        packed = sum(pltpu.roll(jnp.where(is_head, piece.astype(F32), 0.0), N_FOX_HEADS * i, axis=1)
                     if i else jnp.where(is_head, piece.astype(F32), 0.0)
                     for i, piece in enumerate(_split3(cum * LOG2E)))
        c_ref[0, rows, :] = packed.astype(BF16)


def _gate_cum(gate, b_forget, B):
    return pl.pallas_call(
        _gate_cum_kernel,
        grid=(B,),
        in_specs=[pl.BlockSpec((SEQ, LANES), lambda b: (b, 0)), _resident((1, LANES))],
        out_specs=pl.BlockSpec((1, SEQ, LANES), lambda b: (b, 0, 0)),
        out_shape=jax.ShapeDtypeStruct((B, SEQ, LANES), BF16),
        compiler_params=_params(("parallel",), 16),
        name="gate_cum",
    )(gate, b_forget)


def _dilated_kernel(slopes_ref, q_ref, k_ref, v_ref, o_ref, ob_ref, lb_ref):
    pair = pl.program_id(1)
    row = lax.broadcasted_iota(jnp.int32, (2 * BLOCK, 1), 0)
    slope = jnp.where(row < BLOCK, slopes_ref[2 * pair], slopes_ref[2 * pair + 1])
    qi = jnp.where(row < BLOCK, row, row - BLOCK)
    lane = lax.broadcasted_iota(jnp.int32, (1, LANES), 1)

    def rows(start, size, dilation):
        if dilation == 1:
            return pl.ds(start, size)
        return pl.ds(start, size, stride=dilation)

    for branch, (window, dilation) in enumerate(DIL_CONFIGS):
        steps = window // dilation
        length = SEQ // dilation
        n_blocks = length // BLOCK

        def bias_table(n_keys, first_key):
            kj = lax.broadcasted_iota(jnp.int32, (1, n_keys), 1) + first_key
            delta = qi - kj
            valid = (delta >= 0) & (delta <= steps)
            dist = (delta * dilation).astype(F32)
            return jnp.where(valid, (-LOG2E) * slope * dist, NEG)

        bias_first = bias_table(BLOCK, 0)
        bias_rest = bias_table(2 * BLOCK, -BLOCK)

        def scores(q_start, k_start, n_keys, bias):
            q2 = q_ref[0, rows(q_start, BLOCK, dilation), :].astype(BF16)
            k2 = k_ref[0, rows(k_start, n_keys, dilation), :].astype(BF16)
            return _dot_nt(_stack_heads(q2), k2) + bias

        def finish(s, q_start, k_start, n_keys):
            v2 = v_ref[0, rows(k_start, n_keys, dilation), :].astype(BF16)
            m = jnp.max(s, axis=-1, keepdims=True)
            p = jnp.exp2(s - m)
            l = jnp.sum(p, axis=-1, keepdims=True)
            o = _dot(p.astype(BF16), v2) / l
            lse = jnp.broadcast_to(m + jnp.log(l) * LOG2E, (2 * BLOCK, LANES))
            ob_ref[branch, rows(q_start, BLOCK, dilation), :] = _unstack_heads(o)
            lb_ref[branch, rows(q_start, BLOCK, dilation), :] = _unstack_heads(lse)

        blocks = []
        for r in range(dilation):
            for n in range(n_blocks):
                q_start = r + n * dilation * BLOCK
                blocks.append((q_start, q_start, BLOCK, bias_first) if n == 0 else
                              (q_start, q_start - dilation * BLOCK, 2 * BLOCK, bias_rest))
        in_flight = []
        for q_start, k_start, n_keys, bias in blocks:
            in_flight.append((scores(q_start, k_start, n_keys, bias), q_start, k_start, n_keys))
            if len(in_flight) > DIL_SKEW:
                finish(*in_flight.pop(0))
        for item in in_flight:
            finish(*item)

    def mix(i, carry):
        chunk = pl.ds(pl.multiple_of(i * MIX_CHUNK, MIX_CHUNK), MIX_CHUNK)
        lses = [lb_ref[b, chunk, :] for b in range(len(DIL_CONFIGS))]
        top = functools.reduce(jnp.maximum, lses)
        ws = [jnp.exp2(l - top) for l in lses]
        num = sum(w * ob_ref[b, chunk, :] for b, w in enumerate(ws))
        o_ref[0, chunk, :] = (num / sum(ws)).astype(o_ref.dtype)
        return carry

    lax.fori_loop(0, SEQ // MIX_CHUNK, mix, 0)


def _dilated(slopes, qa, ka, va):
    B = qa.shape[0]
    blk = pl.BlockSpec((1, SEQ, LANES), lambda b, p: (b, 0, p))
    n_br = len(DIL_CONFIGS)
    return pl.pallas_call(
        _dilated_kernel,
        grid=(B, HEAD_PAIRS),
        in_specs=[pl.BlockSpec(memory_space=pltpu.SMEM), blk, blk, blk],
        out_specs=blk,
        out_shape=jax.ShapeDtypeStruct((B, SEQ, DIL_WIDTH), BF16),
        scratch_shapes=[pltpu.VMEM((n_br, SEQ, LANES), F32),
                        pltpu.VMEM((n_br, SEQ, LANES), F32)],
        compiler_params=_params(("parallel", "parallel"), 32),
        name="dilated",
    )(slopes, qa, ka, va)


def _fox_kernel(qt_ref, k_ref, c_ref, vt_ref, o_ref,
                s_ref, p_ref, m_ref, alpha_ref, acc_ref):
    first_head = pl.program_id(1) * FOX_HEADS_PER_STEP
    heads = range(FOX_HEADS_PER_STEP)
    tile = FOX_TILE
    kpos = lax.broadcasted_iota(jnp.int32, (tile, tile), 0)
    qpos = lax.broadcasted_iota(jnp.int32, (tile, tile), 1)
    causal = kpos <= qpos
    chan = lax.broadcasted_iota(jnp.int32, (LANES, tile), 0)

    def minus_c(head):
        hit = functools.reduce(jnp.logical_or,
                               [chan == head + N_FOX_HEADS * i for i in range(3)])
        return jnp.where(hit, -1.0, 0.0).astype(BF16)

    pick_c = [minus_c(first_head + h) for h in heads]
    ones_rows = jnp.ones((ONES_ROWS, tile), BF16)
    own = [chan < HEAD_DIM, chan >= HEAD_DIM]

    def tile_slice(t):
        return slice(t * tile, (t + 1) * tile)

    def query_columns(i):
        rhs = []
        for h in heads:
            qt = qt_ref[(h // 2) * LANES:(h // 2 + 1) * LANES, tile_slice(i)]
            rhs.append(jnp.concatenate(
                [jnp.where(own[h % 2], qt, jnp.zeros_like(qt)), pick_c[h]], axis=0))
        return rhs

    def store_scores(rhs, j, h):
        lhs = jnp.concatenate([k_ref[0, tile_slice(j), (h // 2) * LANES:(h // 2 + 1) * LANES],
                               c_ref[0, tile_slice(j), :]], axis=1)
        s_ref[h] = _dot(lhs, rhs[h])

    def accumulate(j, h):
        vt = vt_ref[h * HEAD_DIM:(h + 1) * HEAD_DIM, tile_slice(j)]
        vt_ones = jnp.concatenate([vt, ones_rows], axis=0)
        acc_ref[h] = alpha_ref[h] * acc_ref[h] + _dot(vt_ones, p_ref[h])

    def softmax_step(diagonal, h):
        s = jnp.where(causal, s_ref[h], NEG) if diagonal else s_ref[h]
        m = m_ref[h]
        m_new = jnp.maximum(m, jnp.max(s, axis=0, keepdims=True))
        m_ref[h] = m_new
        alpha_ref[h] = jnp.exp2(m - m_new)
        p_ref[h] = jnp.exp2(s - m_new).astype(BF16)

    def step(j_acc, diagonal, rhs, j_scores):
        for h in heads:
            if j_acc is not None:
                accumulate(j_acc, h)
            softmax_step(diagonal, h)
            if rhs is not None:
                store_scores(rhs, j_scores, h)

    def new_query_tile():
        for h in heads:
            m_ref[h] = jnp.full((1, tile), NEG, F32)

    def write_output(i):
        o_t = jnp.concatenate(
            [acc_ref[h, :HEAD_DIM, :] / acc_ref[h, HEAD_DIM:HEAD_DIM + 1, :] for h in heads],
            axis=0)
        o_ref[0, tile_slice(i), :] = o_t.T.astype(o_ref.dtype)

    n_tiles = SEQ // tile
    rhs0 = query_columns(0)
    for h in heads:
        acc_ref[h] = jnp.zeros((HEAD_DIM + ONES_ROWS, tile), F32)
        store_scores(rhs0, 0, h)
    new_query_tile()
    step(None, True, query_columns(1), 0)

    always = pl.program_id(0) >= 0
    for i in range(1, n_tiles):
        @pl.when(always)
        def _(i=i):
            rhs = query_columns(i)
            new_query_tile()
            step(i - 1, False, rhs, 1)
            write_output(i - 1)
            for j in range(1, i):
                step(j - 1, False, rhs, j + 1)
            step(i - 1, True, query_columns(i + 1) if i + 1 < n_tiles else None, 0)

    for h in heads:
        accumulate(n_tiles - 1, h)
    write_output(n_tiles - 1)


def _fox(qf_t, kf, c_pieces, vf_t):
    B = kf.shape[0]
    width = FOX_HEADS_PER_STEP * HEAD_DIM
    rows_blk = pl.BlockSpec((1, SEQ, width), lambda b, g: (b, 0, g))
    cols_blk = pl.BlockSpec((width, SEQ), lambda b, g: (g, b))
    return pl.pallas_call(
        _fox_kernel,
        grid=(B, N_FOX_HEADS // FOX_HEADS_PER_STEP),
        in_specs=[cols_blk, rows_blk, pl.BlockSpec((1, SEQ, LANES), lambda b, g: (b, 0, 0)),
                  cols_blk],
        out_specs=rows_blk,
        out_shape=jax.ShapeDtypeStruct((B, SEQ, FOX_WIDTH), BF16),
        scratch_shapes=[pltpu.VMEM((FOX_HEADS_PER_STEP, FOX_TILE, FOX_TILE), F32),
                        pltpu.VMEM((FOX_HEADS_PER_STEP, FOX_TILE, FOX_TILE), BF16),
                        pltpu.VMEM((FOX_HEADS_PER_STEP, 1, FOX_TILE), F32),
                        pltpu.VMEM((FOX_HEADS_PER_STEP, 1, FOX_TILE), F32),
                        pltpu.VMEM((FOX_HEADS_PER_STEP, HEAD_DIM + ONES_ROWS, FOX_TILE), F32)],
        compiler_params=_params(("parallel", "parallel"), 32),
        name="fox",
    )(qf_t, kf, c_pieces, vf_t)


def _post_mix_kernel(x_ref, ya_ref, yf_ref, km_ref, vm_ref, woa_ref, wob_ref, g_ref,
                     wq_ref, wo_ref, o_ref):
    x = x_ref[0] + _dot(ya_ref[0], woa_ref[...]) + _dot(yf_ref[0], wob_ref[...])
    h = _rmsnorm(x, g_ref[...]).astype(BF16)
    scale = LOG2E / math.sqrt(XATTN_HEAD_DIM)
    q = (_dot(h, wq_ref[...]) * scale).astype(BF16)
    heads = []
    for hd in range(N_XATTN_HEADS):
        cols = slice(hd * XATTN_HEAD_DIM, (hd + 1) * XATTN_HEAD_DIM)
        s = _dot_nt(q[:, cols], km_ref[0, :, cols])
        p = jnp.exp2(s - jnp.max(s, axis=-1, keepdims=True))
        l = jnp.sum(p, axis=-1, keepdims=True)
        heads.append((_dot(p.astype(BF16), vm_ref[0, :, cols]) / l).astype(BF16))
    o_ref[0] = x + _dot(jnp.concatenate(heads, axis=-1), wo_ref[...])


def _post_mix(x, ya, yf, km, vm, w_oa, w_ob, g_xattn, w_xq, w_xo):
    B = x.shape[0]
    tok = lambda b, i: (b, i, 0)
    per_b = lambda b, i: (b, 0, 0)
    return pl.pallas_call(
        _post_mix_kernel,
        grid=(B, SEQ // ROW_TILE),
        in_specs=[pl.BlockSpec((1, ROW_TILE, D_MODEL), tok),
                  pl.BlockSpec((1, ROW_TILE, DIL_WIDTH), tok),
                  pl.BlockSpec((1, ROW_TILE, FOX_WIDTH), tok),
                  pl.BlockSpec((1, N_MEM, D_MODEL), per_b),
                  pl.BlockSpec((1, N_MEM, D_MODEL), per_b),
                  _resident((DIL_WIDTH, D_MODEL)), _resident((FOX_WIDTH, D_MODEL)),
                  _resident((1, D_MODEL)), _resident((D_MODEL, D_MODEL)),
                  _resident((D_MODEL, D_MODEL))],
        out_specs=pl.BlockSpec((1, ROW_TILE, D_MODEL), tok),
        out_shape=jax.ShapeDtypeStruct(x.shape, F32),
        compiler_params=_params(("parallel", "parallel"), 40),
        name="post_mix",
    )(x, ya, yf, km, vm, w_oa, w_ob, g_xattn, w_xq, w_xo)


def _mlp_kernel(x_ref, g_ref, wu_ref, wd_ref, gf_ref, o_ref):
    x = x_ref[...]
    h = _rmsnorm(x, g_ref[...]).astype(BF16)
    y = x
    for c in range(D_FF // FF_CHUNK):
        cols = slice(c * FF_CHUNK, (c + 1) * FF_CHUNK)
        a = jnp.maximum(_dot(h, wu_ref[:, cols]), 0.0)
        y = y + _dot((a * a).astype(BF16), wd_ref[cols, :])
    o_ref[...] = _rmsnorm(y, gf_ref[...])


def _mlp(x2d, g_mlp, w_up, w_down, g_final):
    T = x2d.shape[0]
    row = pl.BlockSpec((ROW_TILE, D_MODEL), lambda i: (i, 0))
    return pl.pallas_call(
        _mlp_kernel,
        grid=(T // ROW_TILE,),
        in_specs=[row, _resident((1, D_MODEL)), _resident((D_MODEL, D_FF)),
                  _resident((D_FF, D_MODEL)), _resident((1, D_MODEL))],
        out_specs=row,
        out_shape=jax.ShapeDtypeStruct(x2d.shape, F32),
        compiler_params=_params(("parallel",), 48),
        name="mlp",
    )(x2d, g_mlp, w_up, w_down, g_final)


def kernel(x, mem, g_mix, w_in, b_forget, w_out, g_xattn, g_mem, w_xq, w_xk, w_xv, w_xo,
           g_mlp, w_up, w_down, g_final):
    B, S, D = x.shape
    assert (S, D) == (SEQ, D_MODEL) and mem.shape == (B, N_MEM, D_MODEL)
    gain = lambda g: g.reshape(1, D_MODEL).astype(F32)
    bf = lambda w: w.astype(BF16)
    slopes = 2.0 ** (-(jnp.arange(1, N_DIL_HEADS + 1, dtype=F32) * (8.0 / N_DIL_HEADS)))

    km, vm = _mem_kv(mem, gain(g_mem), bf(w_xk), bf(w_xv))
    o_qf = 3 * DIL_WIDTH
    o_kf, o_vf = o_qf + FOX_WIDTH, o_qf + 2 * FOX_WIDTH
    pad_heads = lambda a: jnp.pad(a, ((0, 0), (0, LANES - N_FOX_HEADS)))
    qa, ka, va, qf_t, kf, vf_t, gate = _in_proj(
        x.reshape(B * S, D), gain(g_mix), bf(w_in[:, :o_qf]), bf(w_in[:, o_kf:o_vf]),
        bf(w_in[:, o_qf:o_kf].T), bf(w_in[:, o_vf:QKV_WIDTH].T), bf(pad_heads(w_in[:, QKV_WIDTH:])))
    c_pieces = _gate_cum(gate, pad_heads(b_forget.reshape(1, N_FOX_HEADS).astype(F32)), B)
    seq = lambda a: a.reshape(B, S, a.shape[-1])
    ya = _dilated(slopes, seq(qa), seq(ka), seq(va))
    yf = _fox(qf_t, seq(kf), c_pieces, vf_t)
    x = _post_mix(x, ya, yf, km, vm, bf(w_out[:DIL_WIDTH]), bf(w_out[DIL_WIDTH:]),
                  gain(g_xattn), bf(w_xq), bf(w_xo))
    y = _mlp(x.reshape(B * S, D), gain(g_mlp), bf(w_up), bf(w_down), gain(g_final))
    return y.reshape(B, S, D)
```

```python
import functools
import math

import jax
import jax.numpy as jnp
from jax import lax
from jax.experimental import pallas as pl
from jax.experimental.pallas import tpu as pltpu

D_MODEL = 1024
SEQ = 2048
N_MEM = 256
HEAD_DIM = 64
N_DIL_HEADS = 8
N_FOX_HEADS = 8
DIL_WIDTH = N_DIL_HEADS * HEAD_DIM
FOX_WIDTH = N_FOX_HEADS * HEAD_DIM
QKV_WIDTH = 3 * DIL_WIDTH + 3 * FOX_WIDTH
DIL_CONFIGS = ((128, 1), (512, 4), (2048, 16))
BLOCK = 128
N_XATTN_HEADS = 4
XATTN_HEAD_DIM = D_MODEL // N_XATTN_HEADS
D_FF = 4 * D_MODEL
EPS = 1e-6
NEG = -1e30
LOG2E = math.log2(math.e)

LANES = 128
HEAD_PAIRS = N_FOX_HEADS // 2
ROW_TILE = 1024
FOX_TILE = 256
FOX_HEADS_PER_STEP = 8
ONES_ROWS = 16
FF_CHUNK = 1024
MIX_CHUNK = 256
DIL_SKEW = 1

F32 = jnp.float32
BF16 = jnp.bfloat16

_NT = (((1,), (1,)), ((), ()))


def _dot(a, b):
    return jnp.dot(a, b, preferred_element_type=F32)


def _dot_nt(a, b):
    return lax.dot_general(a, b, _NT, preferred_element_type=F32)


def _rmsnorm(x, g):
    return x * lax.rsqrt(jnp.mean(x * x, axis=-1, keepdims=True) + EPS) * g


def _resident(shape):
    zeros = (0,) * len(shape)
    return pl.BlockSpec(shape, lambda *_: zeros, pipeline_mode=pl.Buffered(1))


def _params(semantics, vmem_mib):
    return pltpu.CompilerParams(dimension_semantics=semantics,
                                vmem_limit_bytes=vmem_mib * 1024 * 1024)


def _stack_heads(q2):
    lane = lax.broadcasted_iota(jnp.int32, (1, LANES), 1)
    zero = jnp.zeros_like(q2)
    return jnp.concatenate([jnp.where(lane < HEAD_DIM, q2, zero),
                            jnp.where(lane >= HEAD_DIM, q2, zero)], axis=0)


def _unstack_heads(o):
    rows = o.shape[0] // 2
    lane = lax.broadcasted_iota(jnp.int32, (1, LANES), 1)
    return jnp.where(lane < HEAD_DIM, o[:rows], o[rows:])


def _mem_kv_kernel(mem_ref, g_ref, wk_ref, wv_ref, k_ref, v_ref):
    m = _rmsnorm(mem_ref[0], g_ref[...]).astype(BF16)
    k_ref[0] = _dot(m, wk_ref[...]).astype(BF16)
    v_ref[0] = _dot(m, wv_ref[...]).astype(BF16)


def _mem_kv(mem, g_mem, w_xk, w_xv):
    B = mem.shape[0]
    blk = pl.BlockSpec((1, N_MEM, D_MODEL), lambda b: (b, 0, 0))
    return pl.pallas_call(
        _mem_kv_kernel,
        grid=(B,),
        in_specs=[blk, _resident((1, D_MODEL)), _resident((D_MODEL, D_MODEL)),
                  _resident((D_MODEL, D_MODEL))],
        out_specs=[blk, blk],
        out_shape=[jax.ShapeDtypeStruct((B, N_MEM, D_MODEL), BF16)] * 2,
        compiler_params=_params(("parallel",), 24),
        name="mem_kv",
    )(mem, g_mem, w_xk, w_xv)


def _in_proj_kernel(x_ref, g_ref, wa_ref, wk_ref, wqt_ref, wvt_ref, wg_ref,
                    qa, ka, va, qft, kf, vft, gate):
    h = _rmsnorm(x_ref[...], g_ref[...]).astype(BF16)
    scale = LOG2E / math.sqrt(HEAD_DIM)
    qa[...] = _dot(h, wa_ref[:, :DIL_WIDTH]) * scale
    ka[...] = _dot(h, wa_ref[:, DIL_WIDTH:2 * DIL_WIDTH])
    va[...] = _dot(h, wa_ref[:, 2 * DIL_WIDTH:])
    qft[...] = (_dot_nt(wqt_ref[...], h) * scale).astype(BF16)
    kf[...] = _dot(h, wk_ref[...]).astype(BF16)
    vft[...] = _dot_nt(wvt_ref[...], h).astype(BF16)
    gate[...] = _dot(h, wg_ref[...])


def _in_proj(x2d, g_mix, w_a, w_kf, w_qf_t, w_vf_t, w_gate):
    T = x2d.shape[0]
    row = lambda i: (i, 0)
    col = lambda i: (0, i)
    rows_blk = pl.BlockSpec((ROW_TILE, DIL_WIDTH), row)
    cols_blk = pl.BlockSpec((FOX_WIDTH, ROW_TILE), col)
    return pl.pallas_call(
        _in_proj_kernel,
        grid=(T // ROW_TILE,),
        in_specs=[pl.BlockSpec((ROW_TILE, D_MODEL), row), _resident((1, D_MODEL)),
                  _resident((D_MODEL, 3 * DIL_WIDTH)), _resident((D_MODEL, FOX_WIDTH)),
                  _resident((FOX_WIDTH, D_MODEL)), _resident((FOX_WIDTH, D_MODEL)),
                  _resident((D_MODEL, LANES))],
        out_specs=[rows_blk] * 3 + [cols_blk, rows_blk, cols_blk,
                                    pl.BlockSpec((ROW_TILE, LANES), row)],
        out_shape=[jax.ShapeDtypeStruct((T, DIL_WIDTH), F32)] * 3
        + [jax.ShapeDtypeStruct((FOX_WIDTH, T), BF16), jax.ShapeDtypeStruct((T, FOX_WIDTH), BF16),
           jax.ShapeDtypeStruct((FOX_WIDTH, T), BF16), jax.ShapeDtypeStruct((T, LANES), F32)],
        compiler_params=_params(("parallel",), 52),
        name="in_proj",
    )(x2d, g_mix, w_a, w_kf, w_qf_t, w_vf_t, w_gate)


def _split3(x):
    hi = x.astype(BF16)
    rest = x - hi.astype(F32)
    mid = rest.astype(BF16)
    lo = (rest - mid.astype(F32)).astype(BF16)
    return hi, mid, lo


def _gate_cum_kernel(gate_ref, b_ref, c_ref):
    r = lax.broadcasted_iota(jnp.int32, (BLOCK, BLOCK), 0)
    c = lax.broadcasted_iota(jnp.int32, (BLOCK, BLOCK), 1)
    lower = (r >= c).astype(BF16)
    is_head = lax.broadcasted_iota(jnp.int32, (1, LANES), 1) < N_FOX_HEADS
    carry = jnp.zeros((1, LANES), F32)
    for blk in range(SEQ // BLOCK):
        rows = slice(blk * BLOCK, (blk + 1) * BLOCK)
        z = gate_ref[rows, :] + b_ref[...]
        log_f = jnp.minimum(z, 0.0) - jnp.log1p(jnp.exp(-jnp.abs(z)))
        cum = sum(_dot(lower, piece) for piece in _split3(log_f)) + carry
        carry = cum[BLOCK - 1:BLOCK, :]
        packed = sum(pltpu.roll(jnp.where(is_head, piece.astype(F32), 0.0), N_FOX_HEADS * i, axis=1)
                     if i else jnp.where(is_head, piece.astype(F32), 0.0)
                     for i, piece in enumerate(_split3(cum * LOG2E)))
        c_ref[0, rows, :] = packed.astype(BF16)


def _gate_cum(gate, b_forget, B):
    return pl.pallas_call(
        _gate_cum_kernel,
        grid=(B,),
        in_specs=[pl.BlockSpec((SEQ, LANES), lambda b: (b, 0)), _resident((1, LANES))],
        out_specs=pl.BlockSpec((1, SEQ, LANES), lambda b: (b, 0, 0)),
        out_shape=jax.ShapeDtypeStruct((B, SEQ, LANES), BF16),
        compiler_params=_params(("parallel",), 16),
        name="gate_cum",
    )(gate, b_forget)


def _dilated_kernel(slopes_ref, q_ref, k_ref, v_ref, o_ref, ob_ref, lb_ref):
    pair = pl.program_id(1)
    row = lax.broadcasted_iota(jnp.int32, (2 * BLOCK, 1), 0)
    slope = jnp.where(row < BLOCK, slopes_ref[2 * pair], slopes_ref[2 * pair + 1])
    qi = jnp.where(row < BLOCK, row, row - BLOCK)
    lane = lax.broadcasted_iota(jnp.int32, (1, LANES), 1)

    def rows(start, size, dilation):
        if dilation == 1:
            return pl.ds(start, size)
        return pl.ds(start, size, stride=dilation)

    for branch, (window, dilation) in enumerate(DIL_CONFIGS):
        steps = window // dilation
        length = SEQ // dilation
        n_blocks = length // BLOCK

        def bias_table(n_keys, first_key):
            kj = lax.broadcasted_iota(jnp.int32, (1, n_keys), 1) + first_key
            delta = qi - kj
            valid = (delta >= 0) & (delta <= steps)
            dist = (delta * dilation).astype(F32)
            return jnp.where(valid, (-LOG2E) * slope * dist, NEG)

        bias_first = bias_table(BLOCK, 0)
        bias_rest = bias_table(2 * BLOCK, -BLOCK)

        def scores(q_start, k_start, n_keys, bias):
            q2 = q_ref[0, rows(q_start, BLOCK, dilation), :].astype(BF16)
            k2 = k_ref[0, rows(k_start, n_keys, dilation), :].astype(BF16)
            return _dot_nt(_stack_heads(q2), k2) + bias

        def finish(s, q_start, k_start, n_keys):
            v2 = v_ref[0, rows(k_start, n_keys, dilation), :].astype(BF16)
            m = jnp.max(s, axis=-1, keepdims=True)
            p = jnp.exp2(s - m)
            l = jnp.sum(p, axis=-1, keepdims=True)
            o = _dot(p.astype(BF16), v2) / l
            lse = jnp.broadcast_to(m + jnp.log(l) * LOG2E, (2 * BLOCK, LANES))
            ob_ref[branch, rows(q_start, BLOCK, dilation), :] = _unstack_heads(o)
            lb_ref[branch, rows(q_start, BLOCK, dilation), :] = _unstack_heads(lse)

        blocks = []
        for r in range(dilation):
            for n in range(n_blocks):
                q_start = r + n * dilation * BLOCK
                blocks.append((q_start, q_start, BLOCK, bias_first) if n == 0 else
                              (q_start, q_start - dilation * BLOCK, 2 * BLOCK, bias_rest))
        in_flight = []
        for q_start, k_start, n_keys, bias in blocks:
            in_flight.append((scores(q_start, k_start, n_keys, bias), q_start, k_start, n_keys))
            if len(in_flight) > DIL_SKEW:
                finish(*in_flight.pop(0))
        for item in in_flight:
            finish(*item)

    def mix(i, carry):
        chunk = pl.ds(pl.multiple_of(i * MIX_CHUNK, MIX_CHUNK), MIX_CHUNK)
        lses = [lb_ref[b, chunk, :] for b in range(len(DIL_CONFIGS))]
        top = functools.reduce(jnp.maximum, lses)
        ws = [jnp.exp2(l - top) for l in lses]
        num = sum(w * ob_ref[b, chunk, :] for b, w in enumerate(ws))
        o_ref[0, chunk, :] = (num / sum(ws)).astype(o_ref.dtype)
        return carry

    lax.fori_loop(0, SEQ // MIX_CHUNK, mix, 0)


def _dilated(slopes, qa, ka, va):
    B = qa.shape[0]
    blk = pl.BlockSpec((1, SEQ, LANES), lambda b, p: (b, 0, p))
    n_br = len(DIL_CONFIGS)
    return pl.pallas_call(
        _dilated_kernel,
        grid=(B, HEAD_PAIRS),
        in_specs=[pl.BlockSpec(memory_space=pltpu.SMEM), blk, blk, blk],
        out_specs=blk,
        out_shape=jax.ShapeDtypeStruct((B, SEQ, DIL_WIDTH), BF16),
        scratch_shapes=[pltpu.VMEM((n_br, SEQ, LANES), F32),
                        pltpu.VMEM((n_br, SEQ, LANES), F32)],
        compiler_params=_params(("parallel", "parallel"), 32),
        name="dilated",
    )(slopes, qa, ka, va)


def _fox_kernel(qt_ref, k_ref, c_ref, vt_ref, o_ref,
                s_ref, p_ref, m_ref, alpha_ref, acc_ref):
    first_head = pl.program_id(1) * FOX_HEADS_PER_STEP
    heads = range(FOX_HEADS_PER_STEP)
    tile = FOX_TILE
    kpos = lax.broadcasted_iota(jnp.int32, (tile, tile), 0)
    qpos = lax.broadcasted_iota(jnp.int32, (tile, tile), 1)
    causal = kpos <= qpos
    chan = lax.broadcasted_iota(jnp.int32, (LANES, tile), 0)

    def minus_c(head):
        hit = functools.reduce(jnp.logical_or,
                               [chan == head + N_FOX_HEADS * i for i in range(3)])
        return jnp.where(hit, -1.0, 0.0).astype(BF16)

    pick_c = [minus_c(first_head + h) for h in heads]
    ones_rows = jnp.ones((ONES_ROWS, tile), BF16)
    own = [chan < HEAD_DIM, chan >= HEAD_DIM]

    def tile_slice(t):
        return slice(t * tile, (t + 1) * tile)

    def query_columns(i):
        rhs = []
        for h in heads:
            qt = qt_ref[(h // 2) * LANES:(h // 2 + 1) * LANES, tile_slice(i)]
            rhs.append(jnp.concatenate(
                [jnp.where(own[h % 2], qt, jnp.zeros_like(qt)), pick_c[h]], axis=0))
        return rhs

    def store_scores(rhs, j, h):
        lhs = jnp.concatenate([k_ref[0, tile_slice(j), (h // 2) * LANES:(h // 2 + 1) * LANES],
                               c_ref[0, tile_slice(j), :]], axis=1)
        s_ref[h] = _dot(lhs, rhs[h])

    def accumulate(j, h):
        vt = vt_ref[h * HEAD_DIM:(h + 1) * HEAD_DIM, tile_slice(j)]
        vt_ones = jnp.concatenate([vt, ones_rows], axis=0)
        acc_ref[h] = alpha_ref[h] * acc_ref[h] + _dot(vt_ones, p_ref[h])

    def softmax_step(diagonal, h):
        s = jnp.where(causal, s_ref[h], NEG) if diagonal else s_ref[h]
        m = m_ref[h]
        m_new = jnp.maximum(m, jnp.max(s, axis=0, keepdims=True))
        m_ref[h] = m_new
        alpha_ref[h] = jnp.exp2(m - m_new)
        p_ref[h] = jnp.exp2(s - m_new).astype(BF16)

    def step(j_acc, diagonal, rhs, j_scores):
        for h in heads:
            if j_acc is not None:
                accumulate(j_acc, h)
            softmax_step(diagonal, h)
            if rhs is not None:
                store_scores(rhs, j_scores, h)

    def new_query_tile():
        for h in heads:
            m_ref[h] = jnp.full((1, tile), NEG, F32)

    def write_output(i):
        o_t = jnp.concatenate(
            [acc_ref[h, :HEAD_DIM, :] / acc_ref[h, HEAD_DIM:HEAD_DIM + 1, :] for h in heads],
            axis=0)
        o_ref[0, tile_slice(i), :] = o_t.T.astype(o_ref.dtype)

    n_tiles = SEQ // tile
    rhs0 = query_columns(0)
    for h in heads:
        acc_ref[h] = jnp.zeros((HEAD_DIM + ONES_ROWS, tile), F32)
        store_scores(rhs0, 0, h)
    new_query_tile()
    step(None, True, query_columns(1), 0)

    always = pl.program_id(0) >= 0
    for i in range(1, n_tiles):
        @pl.when(always)
        def _(i=i):
            rhs = query_columns(i)
            new_query_tile()
            step(i - 1, False, rhs, 1)
            write_output(i - 1)
            for j in range(1, i):
                step(j - 1, False, rhs, j + 1)
            step(i - 1, True, query_columns(i + 1) if i + 1 < n_tiles else None, 0)

    for h in heads:
        accumulate(n_tiles - 1, h)
    write_output(n_tiles - 1)


def _fox(qf_t, kf, c_pieces, vf_t):
    B = kf.shape[0]
    width = FOX_HEADS_PER_STEP * HEAD_DIM
    rows_blk = pl.BlockSpec((1, SEQ, width), lambda b, g: (b, 0, g))
    cols_blk = pl.BlockSpec((width, SEQ), lambda b, g: (g, b))
    return pl.pallas_call(
        _fox_kernel,
        grid=(B, N_FOX_HEADS // FOX_HEADS_PER_STEP),
        in_specs=[cols_blk, rows_blk, pl.BlockSpec((1, SEQ, LANES), lambda b, g: (b, 0, 0)),
                  cols_blk],
        out_specs=rows_blk,
        out_shape=jax.ShapeDtypeStruct((B, SEQ, FOX_WIDTH), BF16),
        scratch_shapes=[pltpu.VMEM((FOX_HEADS_PER_STEP, FOX_TILE, FOX_TILE), F32),
                        pltpu.VMEM((FOX_HEADS_PER_STEP, FOX_TILE, FOX_TILE), BF16),
                        pltpu.VMEM((FOX_HEADS_PER_STEP, 1, FOX_TILE), F32),
                        pltpu.VMEM((FOX_HEADS_PER_STEP, 1, FOX_TILE), F32),
                        pltpu.VMEM((FOX_HEADS_PER_STEP, HEAD_DIM + ONES_ROWS, FOX_TILE), F32)],
        compiler_params=_params(("parallel", "parallel"), 32),
        name="fox",
    )(qf_t, kf, c_pieces, vf_t)


def _post_mix_kernel(x_ref, ya_ref, yf_ref, km_ref, vm_ref, woa_ref, wob_ref, g_ref,
                     wq_ref, wo_ref, o_ref):
    x = x_ref[0] + _dot(ya_ref[0], woa_ref[...]) + _dot(yf_ref[0], wob_ref[...])
    h = _rmsnorm(x, g_ref[...]).astype(BF16)
    scale = LOG2E / math.sqrt(XATTN_HEAD_DIM)
    q = (_dot(h, wq_ref[...]) * scale).astype(BF16)
    heads = []
    for hd in range(N_XATTN_HEADS):
        cols = slice(hd * XATTN_HEAD_DIM, (hd + 1) * XATTN_HEAD_DIM)
        s = _dot_nt(q[:, cols], km_ref[0, :, cols])
        p = jnp.exp2(s - jnp.max(s, axis=-1, keepdims=True))
        l = jnp.sum(p, axis=-1, keepdims=True)
        heads.append((_dot(p.astype(BF16), vm_ref[0, :, cols]) / l).astype(BF16))
    o_ref[0] = x + _dot(jnp.concatenate(heads, axis=-1), wo_ref[...])


def _post_mix(x, ya, yf, km, vm, w_oa, w_ob, g_xattn, w_xq, w_xo):
    B = x.shape[0]
    tok = lambda b, i: (b, i, 0)
    per_b = lambda b, i: (b, 0, 0)
    return pl.pallas_call(
        _post_mix_kernel,
        grid=(B, SEQ // ROW_TILE),
        in_specs=[pl.BlockSpec((1, ROW_TILE, D_MODEL), tok),
                  pl.BlockSpec((1, ROW_TILE, DIL_WIDTH), tok),
                  pl.BlockSpec((1, ROW_TILE, FOX_WIDTH), tok),
                  pl.BlockSpec((1, N_MEM, D_MODEL), per_b),
                  pl.BlockSpec((1, N_MEM, D_MODEL), per_b),
                  _resident((DIL_WIDTH, D_MODEL)), _resident((FOX_WIDTH, D_MODEL)),
                  _resident((1, D_MODEL)), _resident((D_MODEL, D_MODEL)),
                  _resident((D_MODEL, D_MODEL))],
        out_specs=pl.BlockSpec((1, ROW_TILE, D_MODEL), tok),
        out_shape=jax.ShapeDtypeStruct(x.shape, F32),
        compiler_params=_params(("parallel", "parallel"), 56),
        name="post_mix",
    )(x, ya, yf, km, vm, w_oa, w_ob, g_xattn, w_xq, w_xo)


def _mlp_kernel(x_ref, g_ref, wu_ref, wd_ref, gf_ref, o_ref):
    x = x_ref[...]
    h = _rmsnorm(x, g_ref[...]).astype(BF16)
    y = x
    for c in range(D_FF // FF_CHUNK):
        cols = slice(c * FF_CHUNK, (c + 1) * FF_CHUNK)
        a = jnp.maximum(_dot(h, wu_ref[:, cols]), 0.0)
        y = y + _dot((a * a).astype(BF16), wd_ref[cols, :])
    o_ref[...] = _rmsnorm(y, gf_ref[...])


def _mlp(x2d, g_mlp, w_up, w_down, g_final):
    T = x2d.shape[0]
    row = pl.BlockSpec((ROW_TILE, D_MODEL), lambda i: (i, 0))
    return pl.pallas_call(
        _mlp_kernel,
        grid=(T // ROW_TILE,),
        in_specs=[row, _resident((1, D_MODEL)), _resident((D_MODEL, D_FF)),
                  _resident((D_FF, D_MODEL)), _resident((1, D_MODEL))],
        out_specs=row,
        out_shape=jax.ShapeDtypeStruct(x2d.shape, F32),
        compiler_params=_params(("parallel",), 58),
        name="mlp",
    )(x2d, g_mlp, w_up, w_down, g_final)


def kernel(x, mem, g_mix, w_in, b_forget, w_out, g_xattn, g_mem, w_xq, w_xk, w_xv, w_xo,
           g_mlp, w_up, w_down, g_final):
    B, S, D = x.shape
    assert (S, D) == (SEQ, D_MODEL) and mem.shape == (B, N_MEM, D_MODEL)
    gain = lambda g: g.reshape(1, D_MODEL).astype(F32)
    bf = lambda w: w.astype(BF16)
    slopes = 2.0 ** (-(jnp.arange(1, N_DIL_HEADS + 1, dtype=F32) * (8.0 / N_DIL_HEADS)))

    km, vm = _mem_kv(mem, gain(g_mem), bf(w_xk), bf(w_xv))
    o_qf = 3 * DIL_WIDTH
    o_kf, o_vf = o_qf + FOX_WIDTH, o_qf + 2 * FOX_WIDTH
    pad_heads = lambda a: jnp.pad(a, ((0, 0), (0, LANES - N_FOX_HEADS)))
    qa, ka, va, qf_t, kf, vf_t, gate = _in_proj(
        x.reshape(B * S, D), gain(g_mix), bf(w_in[:, :o_qf]), bf(w_in[:, o_kf:o_vf]),
        bf(w_in[:, o_qf:o_kf].T), bf(w_in[:, o_vf:QKV_WIDTH].T), bf(pad_heads(w_in[:, QKV_WIDTH:])))
    c_pieces = _gate_cum(gate, pad_heads(b_forget.reshape(1, N_FOX_HEADS).astype(F32)), B)
    seq = lambda a: a.reshape(B, S, a.shape[-1])
    ya = _dilated(slopes, seq(qa), seq(ka), seq(va))
    yf = _fox(qf_t, seq(kf), c_pieces, vf_t)
    x = _post_mix(x, ya, yf, km, vm, bf(w_out[:DIL_WIDTH]), bf(w_out[DIL_WIDTH:]),
                  gain(g_xattn), bf(w_xq), bf(w_xo))
    y = _mlp(x.reshape(B * S, D), gain(g_mlp), bf(w_up), bf(w_down), gain(g_final))
    return y.reshape(B, S, D)
```

```python
import functools
import math

import jax
import jax.numpy as jnp
from jax import lax
from jax.experimental import pallas as pl
from jax.experimental.pallas import tpu as pltpu

D_MODEL = 1024
SEQ = 2048
N_MEM = 256
HEAD_DIM = 64
N_DIL_HEADS = 8
N_FOX_HEADS = 8
DIL_WIDTH = N_DIL_HEADS * HEAD_DIM
FOX_WIDTH = N_FOX_HEADS * HEAD_DIM
QKV_WIDTH = 3 * DIL_WIDTH + 3 * FOX_WIDTH
DIL_CONFIGS = ((128, 1), (512, 4), (2048, 16))
BLOCK = 128
N_XATTN_HEADS = 4
XATTN_HEAD_DIM = D_MODEL // N_XATTN_HEADS
D_FF = 4 * D_MODEL
EPS = 1e-6
NEG = -1e30
LOG2E = math.log2(math.e)

LANES = 128
HEAD_PAIRS = N_FOX_HEADS // 2
ROW_TILE = 1024
FOX_TILE = 256
FOX_HEADS_PER_STEP = 8
ONES_ROWS = 16
FF_CHUNK = 1024
MIX_CHUNK = 256

F32 = jnp.float32
BF16 = jnp.bfloat16

_NT = (((1,), (1,)), ((), ()))


def _dot(a, b):
    return jnp.dot(a, b, preferred_element_type=F32)


def _dot_nt(a, b):
    return lax.dot_general(a, b, _NT, preferred_element_type=F32)


def _rmsnorm(x, g):
    return x * lax.rsqrt(jnp.mean(x * x, axis=-1, keepdims=True) + EPS) * g


def _resident(shape):
    zeros = (0,) * len(shape)
    return pl.BlockSpec(shape, lambda *_: zeros, pipeline_mode=pl.Buffered(1))


VMEM_MIB = {"mem_kv": 24, "in_proj": 52, "gate_cum": 16, "dilated": 32, "fox": 32,
            "post_mix": 56, "mlp": 58}


def _call_params(name, semantics):
    return dict(name=name, compiler_params=pltpu.CompilerParams(
        dimension_semantics=semantics, vmem_limit_bytes=VMEM_MIB[name] * 1024 * 1024))


def _stack_heads(q2):
    lane = lax.broadcasted_iota(jnp.int32, (1, LANES), 1)
    zero = jnp.zeros_like(q2)
    return jnp.concatenate([jnp.where(lane < HEAD_DIM, q2, zero),
                            jnp.where(lane >= HEAD_DIM, q2, zero)], axis=0)


def _unstack_heads(o):
    rows = o.shape[0] // 2
    lane = lax.broadcasted_iota(jnp.int32, (1, LANES), 1)
    return jnp.where(lane < HEAD_DIM, o[:rows], o[rows:])


def _mem_kv_kernel(mem_ref, g_ref, wk_ref, wv_ref, k_ref, v_ref):
    m = _rmsnorm(mem_ref[0], g_ref[...]).astype(BF16)
    k_ref[0] = _dot(m, wk_ref[...]).astype(BF16)
    v_ref[0] = _dot(m, wv_ref[...]).astype(BF16)


def _mem_kv(mem, g_mem, w_xk, w_xv):
    B = mem.shape[0]
    blk = pl.BlockSpec((1, N_MEM, D_MODEL), lambda b: (b, 0, 0))
    return pl.pallas_call(
        _mem_kv_kernel,
        grid=(B,),
        in_specs=[blk, _resident((1, D_MODEL)), _resident((D_MODEL, D_MODEL)),
                  _resident((D_MODEL, D_MODEL))],
        out_specs=[blk, blk],
        out_shape=[jax.ShapeDtypeStruct((B, N_MEM, D_MODEL), BF16)] * 2,
        **_call_params("mem_kv", ("parallel",)),
    )(mem, g_mem, w_xk, w_xv)


def _in_proj_kernel(x_ref, g_ref, wa_ref, wk_ref, wqt_ref, wvt_ref, wg_ref,
                    qa, ka, va, qft, kf, vft, gate):
    h = _rmsnorm(x_ref[...], g_ref[...]).astype(BF16)
    scale = LOG2E / math.sqrt(HEAD_DIM)
    qa[...] = _dot(h, wa_ref[:, :DIL_WIDTH]) * scale
    ka[...] = _dot(h, wa_ref[:, DIL_WIDTH:2 * DIL_WIDTH])
    va[...] = _dot(h, wa_ref[:, 2 * DIL_WIDTH:])
    qft[...] = (_dot_nt(wqt_ref[...], h) * scale).astype(BF16)
    kf[...] = _dot(h, wk_ref[...]).astype(BF16)
    vft[...] = _dot_nt(wvt_ref[...], h).astype(BF16)
    gate[...] = _dot(h, wg_ref[...])


def _in_proj(x2d, g_mix, w_a, w_kf, w_qf_t, w_vf_t, w_gate):
    T = x2d.shape[0]
    row = lambda i: (i, 0)
    col = lambda i: (0, i)
    rows_blk = pl.BlockSpec((ROW_TILE, DIL_WIDTH), row)
    cols_blk = pl.BlockSpec((FOX_WIDTH, ROW_TILE), col)
    return pl.pallas_call(
        _in_proj_kernel,
        grid=(T // ROW_TILE,),
        in_specs=[pl.BlockSpec((ROW_TILE, D_MODEL), row), _resident((1, D_MODEL)),
                  _resident((D_MODEL, 3 * DIL_WIDTH)), _resident((D_MODEL, FOX_WIDTH)),
                  _resident((FOX_WIDTH, D_MODEL)), _resident((FOX_WIDTH, D_MODEL)),
                  _resident((D_MODEL, LANES))],
        out_specs=[rows_blk] * 3 + [cols_blk, rows_blk, cols_blk,
                                    pl.BlockSpec((ROW_TILE, LANES), row)],
        out_shape=[jax.ShapeDtypeStruct((T, DIL_WIDTH), F32)] * 3
        + [jax.ShapeDtypeStruct((FOX_WIDTH, T), BF16), jax.ShapeDtypeStruct((T, FOX_WIDTH), BF16),
           jax.ShapeDtypeStruct((FOX_WIDTH, T), BF16), jax.ShapeDtypeStruct((T, LANES), F32)],
        **_call_params("in_proj", ("parallel",)),
    )(x2d, g_mix, w_a, w_kf, w_qf_t, w_vf_t, w_gate)


def _split3(x):
    hi = x.astype(BF16)
    rest = x - hi.astype(F32)
    mid = rest.astype(BF16)
    lo = (rest - mid.astype(F32)).astype(BF16)
    return hi, mid, lo


def _gate_cum_kernel(gate_ref, b_ref, c_ref):
    r = lax.broadcasted_iota(jnp.int32, (BLOCK, BLOCK), 0)
    c = lax.broadcasted_iota(jnp.int32, (BLOCK, BLOCK), 1)
    lower = (r >= c).astype(BF16)
    is_head = lax.broadcasted_iota(jnp.int32, (1, LANES), 1) < N_FOX_HEADS
    carry = jnp.zeros((1, LANES), F32)
    for blk in range(SEQ // BLOCK):
        rows = slice(blk * BLOCK, (blk + 1) * BLOCK)
        z = gate_ref[rows, :] + b_ref[...]
        log_f = jnp.minimum(z, 0.0) - jnp.log1p(jnp.exp(-jnp.abs(z)))
        cum = sum(_dot(lower, piece) for piece in _split3(log_f)) + carry
        carry = cum[BLOCK - 1:BLOCK, :]
        packed = sum(pltpu.roll(jnp.where(is_head, piece.astype(F32), 0.0), N_FOX_HEADS * i, axis=1)
                     if i else jnp.where(is_head, piece.astype(F32), 0.0)
                     for i, piece in enumerate(_split3(cum * LOG2E)))
        c_ref[0, rows, :] = packed.astype(BF16)


def _gate_cum(gate, b_forget, B):
    return pl.pallas_call(
        _gate_cum_kernel,
        grid=(B,),
        in_specs=[pl.BlockSpec((SEQ, LANES), lambda b: (b, 0)), _resident((1, LANES))],
        out_specs=pl.BlockSpec((1, SEQ, LANES), lambda b: (b, 0, 0)),
        out_shape=jax.ShapeDtypeStruct((B, SEQ, LANES), BF16),
        **_call_params("gate_cum", ("parallel",)),
    )(gate, b_forget)


def _dilated_kernel(slopes_ref, q_ref, k_ref, v_ref, o_ref, ob_ref, lb_ref):
    pair = pl.program_id(1)
    row = lax.broadcasted_iota(jnp.int32, (2 * BLOCK, 1), 0)
    slope = jnp.where(row < BLOCK, slopes_ref[2 * pair], slopes_ref[2 * pair + 1])
    qi = jnp.where(row < BLOCK, row, row - BLOCK)

    def rows(start, size, dilation):
        if dilation == 1:
            return pl.ds(start, size)
        return pl.ds(start, size, stride=dilation)

    for branch, (window, dilation) in enumerate(DIL_CONFIGS):
        steps = window // dilation
        length = SEQ // dilation
        n_blocks = length // BLOCK

        def bias_table(n_keys, first_key):
            kj = lax.broadcasted_iota(jnp.int32, (1, n_keys), 1) + first_key
            delta = qi - kj
            valid = (delta >= 0) & (delta <= steps)
            dist = (delta * dilation).astype(F32)
            return jnp.where(valid, (-LOG2E) * slope * dist, NEG)

        bias_first = bias_table(BLOCK, 0)
        bias_rest = bias_table(2 * BLOCK, -BLOCK)

        def attend(q_start, k_start, n_keys, bias):
            q2 = q_ref[0, rows(q_start, BLOCK, dilation), :].astype(BF16)
            k2 = k_ref[0, rows(k_start, n_keys, dilation), :].astype(BF16)
            v2 = v_ref[0, rows(k_start, n_keys, dilation), :].astype(BF16)
            s = _dot_nt(_stack_heads(q2), k2) + bias
            m = jnp.max(s, axis=-1, keepdims=True)
            p = jnp.exp2(s - m).astype(BF16)
            pv = _dot(p, jnp.concatenate([v2, jnp.ones_like(v2)], axis=1))
            l = pv[:, LANES:]
            o = pv[:, :LANES] / l
            lse = m + jnp.log(l) * LOG2E
            ob_ref[branch, rows(q_start, BLOCK, dilation), :] = _unstack_heads(o)
            lb_ref[branch, rows(q_start, BLOCK, dilation), :] = _unstack_heads(lse)

        for r in range(dilation):
            for n in range(n_blocks):
                q_start = r + n * dilation * BLOCK
                if n == 0:
                    attend(q_start, q_start, BLOCK, bias_first)
                else:
                    attend(q_start, q_start - dilation * BLOCK, 2 * BLOCK, bias_rest)

    def mix(i, carry):
        chunk = pl.ds(pl.multiple_of(i * MIX_CHUNK, MIX_CHUNK), MIX_CHUNK)
        lses = [lb_ref[b, chunk, :] for b in range(len(DIL_CONFIGS))]
        top = functools.reduce(jnp.maximum, lses)
        ws = [jnp.exp2(l - top) for l in lses]
        num = sum(w * ob_ref[b, chunk, :] for b, w in enumerate(ws))
        o_ref[0, chunk, :] = (num / sum(ws)).astype(o_ref.dtype)
        return carry

    lax.fori_loop(0, SEQ // MIX_CHUNK, mix, 0)


def _dilated(slopes, qa, ka, va):
    B = qa.shape[0]
    blk = pl.BlockSpec((1, SEQ, LANES), lambda b, p: (b, 0, p))
    n_br = len(DIL_CONFIGS)
    return pl.pallas_call(
        _dilated_kernel,
        grid=(B, HEAD_PAIRS),
        in_specs=[pl.BlockSpec(memory_space=pltpu.SMEM), blk, blk, blk],
        out_specs=blk,
        out_shape=jax.ShapeDtypeStruct((B, SEQ, DIL_WIDTH), BF16),
        scratch_shapes=[pltpu.VMEM((n_br, SEQ, LANES), F32),
                        pltpu.VMEM((n_br, SEQ, LANES), F32)],
        **_call_params("dilated", ("parallel", "parallel")),
    )(slopes, qa, ka, va)


def _fox_kernel(qt_ref, k_ref, c_ref, vt_ref, o_ref,
                s_ref, p_ref, m_ref, alpha_ref, acc_ref):
    first_head = pl.program_id(1) * FOX_HEADS_PER_STEP
    heads = range(FOX_HEADS_PER_STEP)
    tile = FOX_TILE
    kpos = lax.broadcasted_iota(jnp.int32, (tile, tile), 0)
    qpos = lax.broadcasted_iota(jnp.int32, (tile, tile), 1)
    causal = kpos <= qpos
    chan = lax.broadcasted_iota(jnp.int32, (LANES, tile), 0)

    def minus_c(head):
        hit = functools.reduce(jnp.logical_or,
                               [chan == head + N_FOX_HEADS * i for i in range(3)])
        return jnp.where(hit, -1.0, 0.0).astype(BF16)

    pick_c = [minus_c(first_head + h) for h in heads]
    ones_rows = jnp.ones((ONES_ROWS, tile), BF16)
    own = [chan < HEAD_DIM, chan >= HEAD_DIM]

    def tile_slice(t):
        return slice(t * tile, (t + 1) * tile)

    def query_columns(i):
        rhs = []
        for h in heads:
            qt = qt_ref[(h // 2) * LANES:(h // 2 + 1) * LANES, tile_slice(i)]
            rhs.append(jnp.concatenate(
                [jnp.where(own[h % 2], qt, jnp.zeros_like(qt)), pick_c[h]], axis=0))
        return rhs

    def store_scores(rhs, j, h):
        lhs = jnp.concatenate([k_ref[0, tile_slice(j), (h // 2) * LANES:(h // 2 + 1) * LANES],
                               c_ref[0, tile_slice(j), :]], axis=1)
        s_ref[h] = _dot(lhs, rhs[h])

    def accumulate(j, h):
        vt = vt_ref[h * HEAD_DIM:(h + 1) * HEAD_DIM, tile_slice(j)]
        vt_ones = jnp.concatenate([vt, ones_rows], axis=0)
        acc_ref[h] = alpha_ref[h] * acc_ref[h] + _dot(vt_ones, p_ref[h])

    def softmax_step(diagonal, h):
        s = jnp.where(causal, s_ref[h], NEG) if diagonal else s_ref[h]
        m = m_ref[h]
        m_new = jnp.maximum(m, jnp.max(s, axis=0, keepdims=True))
        m_ref[h] = m_new
        alpha_ref[h] = jnp.exp2(m - m_new)
        p_ref[h] = jnp.exp2(s - m_new).astype(BF16)

    def step(j_acc, diagonal, rhs, j_scores):
        for h in heads:
            if j_acc is not None:
                accumulate(j_acc, h)
            softmax_step(diagonal, h)
            if rhs is not None:
                store_scores(rhs, j_scores, h)

    def new_query_tile():
        for h in heads:
            m_ref[h] = jnp.full((1, tile), NEG, F32)

    def write_output(i):
        o_t = jnp.concatenate(
            [acc_ref[h, :HEAD_DIM, :] / acc_ref[h, HEAD_DIM:HEAD_DIM + 1, :] for h in heads],
            axis=0)
        o_ref[0, tile_slice(i), :] = o_t.T.astype(o_ref.dtype)

    n_tiles = SEQ // tile
    rhs0 = query_columns(0)
    for h in heads:
        acc_ref[h] = jnp.zeros((HEAD_DIM + ONES_ROWS, tile), F32)
        store_scores(rhs0, 0, h)
    new_query_tile()
    step(None, True, query_columns(1), 0)

    always = pl.program_id(0) >= 0
    for i in range(1, n_tiles):
        @pl.when(always)
        def _(i=i):
            rhs = query_columns(i)
            new_query_tile()
            step(i - 1, False, rhs, 1)
            write_output(i - 1)
            for j in range(1, i):
                step(j - 1, False, rhs, j + 1)
            step(i - 1, True, query_columns(i + 1) if i + 1 < n_tiles else None, 0)

    for h in heads:
        accumulate(n_tiles - 1, h)
    write_output(n_tiles - 1)


def _fox(qf_t, kf, c_pieces, vf_t):
    B = kf.shape[0]
    width = FOX_HEADS_PER_STEP * HEAD_DIM
    rows_blk = pl.BlockSpec((1, SEQ, width), lambda b, g: (b, 0, g))
    cols_blk = pl.BlockSpec((width, SEQ), lambda b, g: (g, b))
    return pl.pallas_call(
        _fox_kernel,
        grid=(B, N_FOX_HEADS // FOX_HEADS_PER_STEP),
        in_specs=[cols_blk, rows_blk, pl.BlockSpec((1, SEQ, LANES), lambda b, g: (b, 0, 0)),
                  cols_blk],
        out_specs=rows_blk,
        out_shape=jax.ShapeDtypeStruct((B, SEQ, FOX_WIDTH), BF16),
        scratch_shapes=[pltpu.VMEM((FOX_HEADS_PER_STEP, FOX_TILE, FOX_TILE), F32),
                        pltpu.VMEM((FOX_HEADS_PER_STEP, FOX_TILE, FOX_TILE), BF16),
                        pltpu.VMEM((FOX_HEADS_PER_STEP, 1, FOX_TILE), F32),
                        pltpu.VMEM((FOX_HEADS_PER_STEP, 1, FOX_TILE), F32),
                        pltpu.VMEM((FOX_HEADS_PER_STEP, HEAD_DIM + ONES_ROWS, FOX_TILE), F32)],
        **_call_params("fox", ("parallel", "parallel")),
    )(qf_t, kf, c_pieces, vf_t)


def _post_mix_kernel(x_ref, ya_ref, yf_ref, km_ref, vm_ref, woa_ref, wob_ref, g_ref,
                     wq_ref, wo_ref, o_ref):
    x = x_ref[0] + _dot(ya_ref[0], woa_ref[...]) + _dot(yf_ref[0], wob_ref[...])
    h = _rmsnorm(x, g_ref[...]).astype(BF16)
    scale = LOG2E / math.sqrt(XATTN_HEAD_DIM)
    q = (_dot(h, wq_ref[...]) * scale).astype(BF16)
    heads = []
    for hd in range(N_XATTN_HEADS):
        cols = slice(hd * XATTN_HEAD_DIM, (hd + 1) * XATTN_HEAD_DIM)
        s = _dot_nt(q[:, cols], km_ref[0, :, cols])
        p = jnp.exp2(s - jnp.max(s, axis=-1, keepdims=True))
        l = jnp.sum(p, axis=-1, keepdims=True)
        heads.append((_dot(p.astype(BF16), vm_ref[0, :, cols]) / l).astype(BF16))
    o_ref[0] = x + _dot(jnp.concatenate(heads, axis=-1), wo_ref[...])


def _post_mix(x, ya, yf, km, vm, w_oa, w_ob, g_xattn, w_xq, w_xo):
    B = x.shape[0]
    tok = lambda b, i: (b, i, 0)
    per_b = lambda b, i: (b, 0, 0)
    return pl.pallas_call(
        _post_mix_kernel,
        grid=(B, SEQ // ROW_TILE),
        in_specs=[pl.BlockSpec((1, ROW_TILE, D_MODEL), tok),
                  pl.BlockSpec((1, ROW_TILE, DIL_WIDTH), tok),
                  pl.BlockSpec((1, ROW_TILE, FOX_WIDTH), tok),
                  pl.BlockSpec((1, N_MEM, D_MODEL), per_b),
                  pl.BlockSpec((1, N_MEM, D_MODEL), per_b),
                  _resident((DIL_WIDTH, D_MODEL)), _resident((FOX_WIDTH, D_MODEL)),
                  _resident((1, D_MODEL)), _resident((D_MODEL, D_MODEL)),
                  _resident((D_MODEL, D_MODEL))],
        out_specs=pl.BlockSpec((1, ROW_TILE, D_MODEL), tok),
        out_shape=jax.ShapeDtypeStruct(x.shape, F32),
        **_call_params("post_mix", ("parallel", "parallel")),
    )(x, ya, yf, km, vm, w_oa, w_ob, g_xattn, w_xq, w_xo)


def _mlp_kernel(x_ref, g_ref, wu_ref, wd_ref, gf_ref, o_ref):
    x = x_ref[...]
    h = _rmsnorm(x, g_ref[...]).astype(BF16)
    y = x
    for c in range(D_FF // FF_CHUNK):
        cols = slice(c * FF_CHUNK, (c + 1) * FF_CHUNK)
        a = jnp.maximum(_dot(h, wu_ref[:, cols]), 0.0)
        y = y + _dot((a * a).astype(BF16), wd_ref[cols, :])
    o_ref[...] = _rmsnorm(y, gf_ref[...])


def _mlp(x2d, g_mlp, w_up, w_down, g_final):
    T = x2d.shape[0]
    row = pl.BlockSpec((ROW_TILE, D_MODEL), lambda i: (i, 0))
    return pl.pallas_call(
        _mlp_kernel,
        grid=(T // ROW_TILE,),
        in_specs=[row, _resident((1, D_MODEL)), _resident((D_MODEL, D_FF)),
                  _resident((D_FF, D_MODEL)), _resident((1, D_MODEL))],
        out_specs=row,
        out_shape=jax.ShapeDtypeStruct(x2d.shape, F32),
        **_call_params("mlp", ("parallel",)),
    )(x2d, g_mlp, w_up, w_down, g_final)


def kernel(x, mem, g_mix, w_in, b_forget, w_out, g_xattn, g_mem, w_xq, w_xk, w_xv, w_xo,
           g_mlp, w_up, w_down, g_final):
    B, S, D = x.shape
    assert (S, D) == (SEQ, D_MODEL) and mem.shape == (B, N_MEM, D_MODEL)
    gain = lambda g: g.reshape(1, D_MODEL).astype(F32)
    bf = lambda w: w.astype(BF16)
    slopes = 2.0 ** (-(jnp.arange(1, N_DIL_HEADS + 1, dtype=F32) * (8.0 / N_DIL_HEADS)))

    km, vm = _mem_kv(mem, gain(g_mem), bf(w_xk), bf(w_xv))
    o_qf = 3 * DIL_WIDTH
    o_kf, o_vf = o_qf + FOX_WIDTH, o_qf + 2 * FOX_WIDTH
    pad_heads = lambda a: jnp.pad(a, ((0, 0), (0, LANES - N_FOX_HEADS)))
    qa, ka, va, qf_t, kf, vf_t, gate = _in_proj(
        x.reshape(B * S, D), gain(g_mix), bf(w_in[:, :o_qf]), bf(w_in[:, o_kf:o_vf]),
        bf(w_in[:, o_qf:o_kf].T), bf(w_in[:, o_vf:QKV_WIDTH].T), bf(pad_heads(w_in[:, QKV_WIDTH:])))
    c_pieces = _gate_cum(gate, pad_heads(b_forget.reshape(1, N_FOX_HEADS).astype(F32)), B)
    seq = lambda a: a.reshape(B, S, a.shape[-1])
    ya = _dilated(slopes, seq(qa), seq(ka), seq(va))
    yf = _fox(qf_t, seq(kf), c_pieces, vf_t)
    x = _post_mix(x, ya, yf, km, vm, bf(w_out[:DIL_WIDTH]), bf(w_out[DIL_WIDTH:]),
                  gain(g_xattn), bf(w_xq), bf(w_xo))
    y = _mlp(x.reshape(B * S, D), gain(g_mlp), bf(w_up), bf(w_down), gain(g_final))
    return y.reshape(B, S, D)
```

```python
import functools
import math

import jax
import jax.numpy as jnp
from jax import lax
from jax.experimental import pallas as pl
from jax.experimental.pallas import tpu as pltpu

D_MODEL = 1024
SEQ = 2048
N_MEM = 256
HEAD_DIM = 64
N_DIL_HEADS = 8
N_FOX_HEADS = 8
DIL_WIDTH = N_DIL_HEADS * HEAD_DIM
FOX_WIDTH = N_FOX_HEADS * HEAD_DIM
QKV_WIDTH = 3 * DIL_WIDTH + 3 * FOX_WIDTH
DIL_CONFIGS = ((128, 1), (512, 4), (2048, 16))
BLOCK = 128
N_XATTN_HEADS = 4
XATTN_HEAD_DIM = D_MODEL // N_XATTN_HEADS
D_FF = 4 * D_MODEL
EPS = 1e-6
NEG = -1e30
LOG2E = math.log2(math.e)

LANES = 128
HEAD_PAIRS = N_FOX_HEADS // 2
ROW_TILE = 1024
FOX_TILE = 256
FOX_HEADS_PER_STEP = 8
ONES_ROWS = 16
FF_CHUNK = 1024
MIX_CHUNK = 256

F32 = jnp.float32
BF16 = jnp.bfloat16

_NT = (((1,), (1,)), ((), ()))


def _dot(a, b):
    return jnp.dot(a, b, preferred_element_type=F32)


def _dot_nt(a, b):
    return lax.dot_general(a, b, _NT, preferred_element_type=F32)


def _rmsnorm(x, g):
    return x * lax.rsqrt(jnp.mean(x * x, axis=-1, keepdims=True) + EPS) * g


def _resident(shape):
    zeros = (0,) * len(shape)
    return pl.BlockSpec(shape, lambda *_: zeros, pipeline_mode=pl.Buffered(1))


VMEM_MIB = {"mem_kv": 24, "in_proj": 52, "gate_cum": 16, "dilated": 32, "fox": 32,
            "post_mix": 56, "mlp": 58}


def _call_params(name, semantics):
    return dict(name=name, compiler_params=pltpu.CompilerParams(
        dimension_semantics=semantics, vmem_limit_bytes=VMEM_MIB[name] * 1024 * 1024))


def _stack_heads(q2):
    lane = lax.broadcasted_iota(jnp.int32, (1, LANES), 1)
    zero = jnp.zeros_like(q2)
    return jnp.concatenate([jnp.where(lane < HEAD_DIM, q2, zero),
                            jnp.where(lane >= HEAD_DIM, q2, zero)], axis=0)


def _unstack_heads(o):
    rows = o.shape[0] // 2
    lane = lax.broadcasted_iota(jnp.int32, (1, LANES), 1)
    return jnp.where(lane < HEAD_DIM, o[:rows], o[rows:])


def _mem_kv_kernel(mem_ref, g_ref, wk_ref, wv_ref, k_ref, v_ref):
    m = _rmsnorm(mem_ref[0], g_ref[...]).astype(BF16)
    k_ref[0] = _dot(m, wk_ref[...]).astype(BF16)
    v_ref[0] = _dot(m, wv_ref[...]).astype(BF16)


def _mem_kv(mem, g_mem, w_xk, w_xv):
    B = mem.shape[0]
    blk = pl.BlockSpec((1, N_MEM, D_MODEL), lambda b: (b, 0, 0))
    return pl.pallas_call(
        _mem_kv_kernel,
        grid=(B,),
        in_specs=[blk, _resident((1, D_MODEL)), _resident((D_MODEL, D_MODEL)),
                  _resident((D_MODEL, D_MODEL))],
        out_specs=[blk, blk],
        out_shape=[jax.ShapeDtypeStruct((B, N_MEM, D_MODEL), BF16)] * 2,
        **_call_params("mem_kv", ("parallel",)),
    )(mem, g_mem, w_xk, w_xv)


def _in_proj_kernel(x_ref, g_ref, wa_ref, wk_ref, wqt_ref, wvt_ref, wg_ref,
                    qa, ka, va, qft, kf, vft, gate):
    h = _rmsnorm(x_ref[...], g_ref[...]).astype(BF16)
    scale = LOG2E / math.sqrt(HEAD_DIM)
    qa[...] = _dot(h, wa_ref[:, :DIL_WIDTH]) * scale
    ka[...] = _dot(h, wa_ref[:, DIL_WIDTH:2 * DIL_WIDTH])
    va[...] = _dot(h, wa_ref[:, 2 * DIL_WIDTH:])
    qft[...] = (_dot_nt(wqt_ref[...], h) * scale).astype(BF16)
    kf[...] = _dot(h, wk_ref[...]).astype(BF16)
    vft[...] = _dot_nt(wvt_ref[...], h).astype(BF16)
    gate[...] = _dot(h, wg_ref[...])


def _in_proj(x2d, g_mix, w_a, w_kf, w_qf_t, w_vf_t, w_gate):
    T = x2d.shape[0]
    row = lambda i: (i, 0)
    col = lambda i: (0, i)
    rows_blk = pl.BlockSpec((ROW_TILE, DIL_WIDTH), row)
    cols_blk = pl.BlockSpec((FOX_WIDTH, ROW_TILE), col)
    return pl.pallas_call(
        _in_proj_kernel,
        grid=(T // ROW_TILE,),
        in_specs=[pl.BlockSpec((ROW_TILE, D_MODEL), row), _resident((1, D_MODEL)),
                  _resident((D_MODEL, 3 * DIL_WIDTH)), _resident((D_MODEL, FOX_WIDTH)),
                  _resident((FOX_WIDTH, D_MODEL)), _resident((FOX_WIDTH, D_MODEL)),
                  _resident((D_MODEL, LANES))],
        out_specs=[rows_blk] * 3 + [cols_blk, rows_blk, cols_blk,
                                    pl.BlockSpec((ROW_TILE, LANES), row)],
        out_shape=[jax.ShapeDtypeStruct((T, DIL_WIDTH), F32)] * 3
        + [jax.ShapeDtypeStruct((FOX_WIDTH, T), BF16), jax.ShapeDtypeStruct((T, FOX_WIDTH), BF16),
           jax.ShapeDtypeStruct((FOX_WIDTH, T), BF16), jax.ShapeDtypeStruct((T, LANES), F32)],
        **_call_params("in_proj", ("parallel",)),
    )(x2d, g_mix, w_a, w_kf, w_qf_t, w_vf_t, w_gate)


def _split3(x):
    hi = x.astype(BF16)
    rest = x - hi.astype(F32)
    mid = rest.astype(BF16)
    lo = (rest - mid.astype(F32)).astype(BF16)
    return hi, mid, lo


def _gate_cum_kernel(gate_ref, b_ref, c_ref):
    r = lax.broadcasted_iota(jnp.int32, (BLOCK, BLOCK), 0)
    c = lax.broadcasted_iota(jnp.int32, (BLOCK, BLOCK), 1)
    lower = (r >= c).astype(BF16)
    is_head = lax.broadcasted_iota(jnp.int32, (1, LANES), 1) < N_FOX_HEADS
    carry = jnp.zeros((1, LANES), F32)
    for blk in range(SEQ // BLOCK):
        rows = slice(blk * BLOCK, (blk + 1) * BLOCK)
        z = gate_ref[rows, :] + b_ref[...]
        log_f = jnp.minimum(z, 0.0) - jnp.log1p(jnp.exp(-jnp.abs(z)))
        cum = sum(_dot(lower, piece) for piece in _split3(log_f)) + carry
        carry = cum[BLOCK - 1:BLOCK, :]
        packed = sum(pltpu.roll(jnp.where(is_head, piece.astype(F32), 0.0), N_FOX_HEADS * i, axis=1)
                     if i else jnp.where(is_head, piece.astype(F32), 0.0)
                     for i, piece in enumerate(_split3(cum * LOG2E)))
        c_ref[0, rows, :] = packed.astype(BF16)


def _gate_cum(gate, b_forget, B):
    return pl.pallas_call(
        _gate_cum_kernel,
        grid=(B,),
        in_specs=[pl.BlockSpec((SEQ, LANES), lambda b: (b, 0)), _resident((1, LANES))],
        out_specs=pl.BlockSpec((1, SEQ, LANES), lambda b: (b, 0, 0)),
        out_shape=jax.ShapeDtypeStruct((B, SEQ, LANES), BF16),
        **_call_params("gate_cum", ("parallel",)),
    )(gate, b_forget)


def _dilated_kernel(slopes_ref, q_ref, k_ref, v_ref, o_ref, ob_ref, lb_ref):
    pair = pl.program_id(1)
    row = lax.broadcasted_iota(jnp.int32, (2 * BLOCK, 1), 0)
    slope = jnp.where(row < BLOCK, slopes_ref[2 * pair], slopes_ref[2 * pair + 1])
    qi = jnp.where(row < BLOCK, row, row - BLOCK)

    def rows(start, size, dilation):
        if dilation == 1:
            return pl.ds(start, size)
        return pl.ds(start, size, stride=dilation)

    for branch, (window, dilation) in enumerate(DIL_CONFIGS):
        steps = window // dilation
        length = SEQ // dilation
        n_blocks = length // BLOCK

        def bias_table(n_keys, first_key):
            kj = lax.broadcasted_iota(jnp.int32, (1, n_keys), 1) + first_key
            delta = qi - kj
            valid = (delta >= 0) & (delta <= steps)
            dist = (delta * dilation).astype(F32)
            return jnp.where(valid, (-LOG2E) * slope * dist, NEG)

        bias_first = bias_table(BLOCK, 0)
        bias_rest = bias_table(2 * BLOCK, -BLOCK)

        def attend(q_start, k_start, n_keys, bias):
            q2 = q_ref[0, rows(q_start, BLOCK, dilation), :].astype(BF16)
            k2 = k_ref[0, rows(k_start, n_keys, dilation), :].astype(BF16)
            v2 = v_ref[0, rows(k_start, n_keys, dilation), :].astype(BF16)
            s = _dot_nt(_stack_heads(q2), k2) + bias
            m = jnp.max(s, axis=-1, keepdims=True)
            p = jnp.exp2(s - m)
            l = jnp.sum(p, axis=-1, keepdims=True)
            o = _dot(p.astype(BF16), v2) / l
            lse = jnp.broadcast_to(m + jnp.log(l) * LOG2E, (2 * BLOCK, LANES))
            ob_ref[branch, rows(q_start, BLOCK, dilation), :] = _unstack_heads(o)
            lb_ref[branch, rows(q_start, BLOCK, dilation), :] = _unstack_heads(lse)

        for r in range(dilation):
            for n in range(n_blocks):
                q_start = r + n * dilation * BLOCK
                if n == 0:
                    attend(q_start, q_start, BLOCK, bias_first)
                else:
                    attend(q_start, q_start - dilation * BLOCK, 2 * BLOCK, bias_rest)

    def mix(i, carry):
        chunk = pl.ds(pl.multiple_of(i * MIX_CHUNK, MIX_CHUNK), MIX_CHUNK)
        lses = [lb_ref[b, chunk, :] for b in range(len(DIL_CONFIGS))]
        top = functools.reduce(jnp.maximum, lses)
        ws = [jnp.exp2(l - top) for l in lses]
        num = sum(w * ob_ref[b, chunk, :] for b, w in enumerate(ws))
        o_ref[0, chunk, :] = (num / sum(ws)).astype(o_ref.dtype)
        return carry

    lax.fori_loop(0, SEQ // MIX_CHUNK, mix, 0)


def _dilated(slopes, qa, ka, va):
    B = qa.shape[0]
    blk = pl.BlockSpec((1, SEQ, LANES), lambda b, p: (b, 0, p))
    n_br = len(DIL_CONFIGS)
    return pl.pallas_call(
        _dilated_kernel,
        grid=(B, HEAD_PAIRS),
        in_specs=[pl.BlockSpec(memory_space=pltpu.SMEM), blk, blk, blk],
        out_specs=blk,
        out_shape=jax.ShapeDtypeStruct((B, SEQ, DIL_WIDTH), BF16),
        scratch_shapes=[pltpu.VMEM((n_br, SEQ, LANES), F32),
                        pltpu.VMEM((n_br, SEQ, LANES), F32)],
        **_call_params("dilated", ("parallel", "parallel")),
    )(slopes, qa, ka, va)


def _fox_kernel(qt_ref, k_ref, c_ref, vt_ref, o_ref,
                s_ref, p_ref, m_ref, alpha_ref, acc_ref):
    first_head = pl.program_id(1) * FOX_HEADS_PER_STEP
    heads = range(FOX_HEADS_PER_STEP)
    tile = FOX_TILE
    kpos = lax.broadcasted_iota(jnp.int32, (tile, tile), 0)
    qpos = lax.broadcasted_iota(jnp.int32, (tile, tile), 1)
    causal = kpos <= qpos
    chan = lax.broadcasted_iota(jnp.int32, (LANES, tile), 0)

    def minus_c(head):
        hit = functools.reduce(jnp.logical_or,
                               [chan == head + N_FOX_HEADS * i for i in range(3)])
        return jnp.where(hit, -1.0, 0.0).astype(BF16)

    pick_c = [minus_c(first_head + h) for h in heads]
    ones_rows = jnp.ones((ONES_ROWS, tile), BF16)
    own = [chan < HEAD_DIM, chan >= HEAD_DIM]

    def tile_slice(t):
        return slice(t * tile, (t + 1) * tile)

    def query_columns(i):
        rhs = []
        for h in heads:
            qt = qt_ref[(h // 2) * LANES:(h // 2 + 1) * LANES, tile_slice(i)]
            rhs.append(jnp.concatenate(
                [jnp.where(own[h % 2], qt, jnp.zeros_like(qt)), pick_c[h]], axis=0))
        return rhs

    def store_scores(rhs, j, h):
        lhs = jnp.concatenate([k_ref[0, tile_slice(j), (h // 2) * LANES:(h // 2 + 1) * LANES],
                               c_ref[0, tile_slice(j), :]], axis=1)
        s_ref[h] = _dot(lhs, rhs[h])

    def accumulate(j, h):
        vt = vt_ref[h * HEAD_DIM:(h + 1) * HEAD_DIM, tile_slice(j)]
        vt_ones = jnp.concatenate([vt, ones_rows], axis=0)
        acc_ref[h] = alpha_ref[h] * acc_ref[h] + _dot(vt_ones, p_ref[h])

    def softmax_step(diagonal, h):
        s = jnp.where(causal, s_ref[h], NEG) if diagonal else s_ref[h]
        m = m_ref[h]
        m_new = jnp.maximum(m, jnp.max(s, axis=0, keepdims=True))
        m_ref[h] = m_new
        alpha_ref[h] = jnp.exp2(m - m_new)
        p_ref[h] = jnp.exp2(s - m_new).astype(BF16)

    def step(j_acc, diagonal, rhs, j_scores):
        for h in heads:
            if j_acc is not None:
                accumulate(j_acc, h)
            softmax_step(diagonal, h)
            if rhs is not None:
                store_scores(rhs, j_scores, h)

    def new_query_tile():
        for h in heads:
            m_ref[h] = jnp.full((1, tile), NEG, F32)

    def write_output(i):
        o_t = jnp.concatenate(
            [acc_ref[h, :HEAD_DIM, :] / acc_ref[h, HEAD_DIM:HEAD_DIM + 1, :] for h in heads],
            axis=0)
        o_ref[0, tile_slice(i), :] = o_t.T.astype(o_ref.dtype)

    n_tiles = SEQ // tile
    rhs0 = query_columns(0)
    for h in heads:
        acc_ref[h] = jnp.zeros((HEAD_DIM + ONES_ROWS, tile), F32)
        store_scores(rhs0, 0, h)
    new_query_tile()
    step(None, True, query_columns(1), 0)

    always = pl.program_id(0) >= 0
    for i in range(1, n_tiles):
        @pl.when(always)
        def _(i=i):
            rhs = query_columns(i)
            new_query_tile()
            step(i - 1, False, rhs, 1)
            write_output(i - 1)
            for j in range(1, i):
                step(j - 1, False, rhs, j + 1)
            step(i - 1, True, query_columns(i + 1) if i + 1 < n_tiles else None, 0)

    for h in heads:
        accumulate(n_tiles - 1, h)
    write_output(n_tiles - 1)


def _fox(qf_t, kf, c_pieces, vf_t):
    B = kf.shape[0]
    width = FOX_HEADS_PER_STEP * HEAD_DIM
    rows_blk = pl.BlockSpec((1, SEQ, width), lambda b, g: (b, 0, g))
    cols_blk = pl.BlockSpec((width, SEQ), lambda b, g: (g, b))
    return pl.pallas_call(
        _fox_kernel,
        grid=(B, N_FOX_HEADS // FOX_HEADS_PER_STEP),
        in_specs=[cols_blk, rows_blk, pl.BlockSpec((1, SEQ, LANES), lambda b, g: (b, 0, 0)),
                  cols_blk],
        out_specs=rows_blk,
        out_shape=jax.ShapeDtypeStruct((B, SEQ, FOX_WIDTH), BF16),
        scratch_shapes=[pltpu.VMEM((FOX_HEADS_PER_STEP, FOX_TILE, FOX_TILE), F32),
                        pltpu.VMEM((FOX_HEADS_PER_STEP, FOX_TILE, FOX_TILE), BF16),
                        pltpu.VMEM((FOX_HEADS_PER_STEP, 1, FOX_TILE), F32),
                        pltpu.VMEM((FOX_HEADS_PER_STEP, 1, FOX_TILE), F32),
                        pltpu.VMEM((FOX_HEADS_PER_STEP, HEAD_DIM + ONES_ROWS, FOX_TILE), F32)],
        **_call_params("fox", ("parallel", "parallel")),
    )(qf_t, kf, c_pieces, vf_t)


def _post_mix_kernel(x_ref, ya_ref, yf_ref, km_ref, vm_ref, woa_ref, wob_ref, g_ref,
                     wq_ref, wo_ref, o_ref):
    x = x_ref[0] + _dot(ya_ref[0], woa_ref[...]) + _dot(yf_ref[0], wob_ref[...])
    h = _rmsnorm(x, g_ref[...]).astype(BF16)
    scale = LOG2E / math.sqrt(XATTN_HEAD_DIM)
    q = (_dot(h, wq_ref[...]) * scale).astype(BF16)
    heads = []
    for hd in range(N_XATTN_HEADS):
        cols = slice(hd * XATTN_HEAD_DIM, (hd + 1) * XATTN_HEAD_DIM)
        s = _dot_nt(q[:, cols], km_ref[0, :, cols])
        p = jnp.exp2(s - jnp.max(s, axis=-1, keepdims=True))
        l = jnp.sum(p, axis=-1, keepdims=True)
        heads.append((_dot(p.astype(BF16), vm_ref[0, :, cols]) / l).astype(BF16))
    o_ref[0] = x + _dot(jnp.concatenate(heads, axis=-1), wo_ref[...])


def _post_mix(x, ya, yf, km, vm, w_oa, w_ob, g_xattn, w_xq, w_xo):
    B = x.shape[0]
    tok = lambda b, i: (b, i, 0)
    per_b = lambda b, i: (b, 0, 0)
    return pl.pallas_call(
        _post_mix_kernel,
        grid=(B, SEQ // ROW_TILE),
        in_specs=[pl.BlockSpec((1, ROW_TILE, D_MODEL), tok),
                  pl.BlockSpec((1, ROW_TILE, DIL_WIDTH), tok),
                  pl.BlockSpec((1, ROW_TILE, FOX_WIDTH), tok),
                  pl.BlockSpec((1, N_MEM, D_MODEL), per_b),
                  pl.BlockSpec((1, N_MEM, D_MODEL), per_b),
                  _resident((DIL_WIDTH, D_MODEL)), _resident((FOX_WIDTH, D_MODEL)),
                  _resident((1, D_MODEL)), _resident((D_MODEL, D_MODEL)),
                  _resident((D_MODEL, D_MODEL))],
        out_specs=pl.BlockSpec((1, ROW_TILE, D_MODEL), tok),
        out_shape=jax.ShapeDtypeStruct(x.shape, F32),
        **_call_params("post_mix", ("parallel", "parallel")),
    )(x, ya, yf, km, vm, w_oa, w_ob, g_xattn, w_xq, w_xo)


def _mlp_kernel(x_ref, g_ref, wu_ref, wd_ref, gf_ref, o_ref):
    x = x_ref[...]
    h = _rmsnorm(x, g_ref[...]).astype(BF16)
    y = x
    for c in range(D_FF // FF_CHUNK):
        cols = slice(c * FF_CHUNK, (c + 1) * FF_CHUNK)
        a = jnp.maximum(_dot(h, wu_ref[:, cols]), 0.0)
        y = y + _dot((a * a).astype(BF16), wd_ref[cols, :])
    o_ref[...] = _rmsnorm(y, gf_ref[...])


def _mlp(x2d, g_mlp, w_up, w_down, g_final):
    T = x2d.shape[0]
    row = pl.BlockSpec((ROW_TILE, D_MODEL), lambda i: (i, 0))
    return pl.pallas_call(
        _mlp_kernel,
        grid=(T // ROW_TILE,),
        in_specs=[row, _resident((1, D_MODEL)), _resident((D_MODEL, D_FF)),
                  _resident((D_FF, D_MODEL)), _resident((1, D_MODEL))],
        out_specs=row,
        out_shape=jax.ShapeDtypeStruct(x2d.shape, F32),
        **_call_params("mlp", ("parallel",)),
    )(x2d, g_mlp, w_up, w_down, g_final)


def kernel(x, mem, g_mix, w_in, b_forget, w_out, g_xattn, g_mem, w_xq, w_xk, w_xv, w_xo,
           g_mlp, w_up, w_down, g_final):
    B, S, D = x.shape
    assert (S, D) == (SEQ, D_MODEL) and mem.shape == (B, N_MEM, D_MODEL)
    gain = lambda g: g.reshape(1, D_MODEL).astype(F32)
    bf = lambda w: w.astype(BF16)
    slopes = 2.0 ** (-(jnp.arange(1, N_DIL_HEADS + 1, dtype=F32) * (8.0 / N_DIL_HEADS)))

    km, vm = _mem_kv(mem, gain(g_mem), bf(w_xk), bf(w_xv))
    o_qf = 3 * DIL_WIDTH
    o_kf, o_vf = o_qf + FOX_WIDTH, o_qf + 2 * FOX_WIDTH
    pad_heads = lambda a: jnp.pad(a, ((0, 0), (0, LANES - N_FOX_HEADS)))
    qa, ka, va, qf_t, kf, vf_t, gate = _in_proj(
        x.reshape(B * S, D), gain(g_mix), bf(w_in[:, :o_qf]), bf(w_in[:, o_kf:o_vf]),
        bf(w_in[:, o_qf:o_kf].T), bf(w_in[:, o_vf:QKV_WIDTH].T), bf(pad_heads(w_in[:, QKV_WIDTH:])))
    c_pieces = _gate_cum(gate, pad_heads(b_forget.reshape(1, N_FOX_HEADS).astype(F32)), B)
    seq = lambda a: a.reshape(B, S, a.shape[-1])
    ya = _dilated(slopes, seq(qa), seq(ka), seq(va))
    yf = _fox(qf_t, seq(kf), c_pieces, vf_t)
    x = _post_mix(x, ya, yf, km, vm, bf(w_out[:DIL_WIDTH]), bf(w_out[DIL_WIDTH:]),
                  gain(g_xattn), bf(w_xq), bf(w_xo))
    y = _mlp(x.reshape(B * S, D), gain(g_mlp), bf(w_up), bf(w_down), gain(g_final))
    return y.reshape(B, S, D)
```

```python
import functools
import math

import jax
import jax.numpy as jnp
from jax import lax
from jax.experimental import pallas as pl
from jax.experimental.pallas import tpu as pltpu

D_MODEL = 1024
SEQ = 2048
N_MEM = 256
HEAD_DIM = 64
N_DIL_HEADS = 8
N_FOX_HEADS = 8
DIL_WIDTH = N_DIL_HEADS * HEAD_DIM
FOX_WIDTH = N_FOX_HEADS * HEAD_DIM
QKV_WIDTH = 3 * DIL_WIDTH + 3 * FOX_WIDTH
DIL_CONFIGS = ((128, 1), (512, 4), (2048, 16))
BLOCK = 128
N_XATTN_HEADS = 4
XATTN_HEAD_DIM = D_MODEL // N_XATTN_HEADS
D_FF = 4 * D_MODEL
EPS = 1e-6
NEG = -1e30
LOG2E = math.log2(math.e)

LANES = 128
HEAD_PAIRS = N_FOX_HEADS // 2
ROW_TILE = 1024
FOX_TILE = 256
FOX_HEADS_PER_STEP = 8
ONES_ROWS = 16
FF_CHUNK = 512
MIX_CHUNK = 256

F32 = jnp.float32
BF16 = jnp.bfloat16

_NT = (((1,), (1,)), ((), ()))


def _dot(a, b):
    return jnp.dot(a, b, preferred_element_type=F32)


def _dot_nt(a, b):
    return lax.dot_general(a, b, _NT, preferred_element_type=F32)


def _rmsnorm(x, g):
    return x * lax.rsqrt(jnp.mean(x * x, axis=-1, keepdims=True) + EPS) * g


def _resident(shape):
    zeros = (0,) * len(shape)
    return pl.BlockSpec(shape, lambda *_: zeros, pipeline_mode=pl.Buffered(1))


VMEM_MIB = {"mem_kv": 24, "in_proj": 52, "gate_cum": 16, "dilated": 32, "fox": 32,
            "post_mix": 56, "mlp": 58}


def _call_params(name, semantics):
    return dict(name=name, compiler_params=pltpu.CompilerParams(
        dimension_semantics=semantics, vmem_limit_bytes=VMEM_MIB[name] * 1024 * 1024))


def _stack_heads(q2):
    lane = lax.broadcasted_iota(jnp.int32, (1, LANES), 1)
    zero = jnp.zeros_like(q2)
    return jnp.concatenate([jnp.where(lane < HEAD_DIM, q2, zero),
                            jnp.where(lane >= HEAD_DIM, q2, zero)], axis=0)


def _unstack_heads(o):
    rows = o.shape[0] // 2
    lane = lax.broadcasted_iota(jnp.int32, (1, LANES), 1)
    return jnp.where(lane < HEAD_DIM, o[:rows], o[rows:])


def _mem_kv_kernel(mem_ref, g_ref, wk_ref, wv_ref, k_ref, v_ref):
    m = _rmsnorm(mem_ref[0], g_ref[...]).astype(BF16)
    k_ref[0] = _dot(m, wk_ref[...]).astype(BF16)
    v_ref[0] = _dot(m, wv_ref[...]).astype(BF16)


def _mem_kv(mem, g_mem, w_xk, w_xv):
    B = mem.shape[0]
    blk = pl.BlockSpec((1, N_MEM, D_MODEL), lambda b: (b, 0, 0))
    return pl.pallas_call(
        _mem_kv_kernel,
        grid=(B,),
        in_specs=[blk, _resident((1, D_MODEL)), _resident((D_MODEL, D_MODEL)),
                  _resident((D_MODEL, D_MODEL))],
        out_specs=[blk, blk],
        out_shape=[jax.ShapeDtypeStruct((B, N_MEM, D_MODEL), BF16)] * 2,
        **_call_params("mem_kv", ("parallel",)),
    )(mem, g_mem, w_xk, w_xv)


def _in_proj_kernel(x_ref, g_ref, wa_ref, wk_ref, wqt_ref, wvt_ref, wg_ref,
                    qa, ka, va, qft, kf, vft, gate):
    h = _rmsnorm(x_ref[...], g_ref[...]).astype(BF16)
    scale = LOG2E / math.sqrt(HEAD_DIM)
    qa[...] = _dot(h, wa_ref[:, :DIL_WIDTH]) * scale
    ka[...] = _dot(h, wa_ref[:, DIL_WIDTH:2 * DIL_WIDTH])
    va[...] = _dot(h, wa_ref[:, 2 * DIL_WIDTH:])
    qft[...] = (_dot_nt(wqt_ref[...], h) * scale).astype(BF16)
    kf[...] = _dot(h, wk_ref[...]).astype(BF16)
    vft[...] = _dot_nt(wvt_ref[...], h).astype(BF16)
    gate[...] = _dot(h, wg_ref[...])


def _in_proj(x2d, g_mix, w_a, w_kf, w_qf_t, w_vf_t, w_gate):
    T = x2d.shape[0]
    row = lambda i: (i, 0)
    col = lambda i: (0, i)
    rows_blk = pl.BlockSpec((ROW_TILE, DIL_WIDTH), row)
    cols_blk = pl.BlockSpec((FOX_WIDTH, ROW_TILE), col)
    return pl.pallas_call(
        _in_proj_kernel,
        grid=(T // ROW_TILE,),
        in_specs=[pl.BlockSpec((ROW_TILE, D_MODEL), row), _resident((1, D_MODEL)),
                  _resident((D_MODEL, 3 * DIL_WIDTH)), _resident((D_MODEL, FOX_WIDTH)),
                  _resident((FOX_WIDTH, D_MODEL)), _resident((FOX_WIDTH, D_MODEL)),
                  _resident((D_MODEL, LANES))],
        out_specs=[rows_blk] * 3 + [cols_blk, rows_blk, cols_blk,
                                    pl.BlockSpec((ROW_TILE, LANES), row)],
        out_shape=[jax.ShapeDtypeStruct((T, DIL_WIDTH), F32)] * 3
        + [jax.ShapeDtypeStruct((FOX_WIDTH, T), BF16), jax.ShapeDtypeStruct((T, FOX_WIDTH), BF16),
           jax.ShapeDtypeStruct((FOX_WIDTH, T), BF16), jax.ShapeDtypeStruct((T, LANES), F32)],
        **_call_params("in_proj", ("parallel",)),
    )(x2d, g_mix, w_a, w_kf, w_qf_t, w_vf_t, w_gate)


def _split3(x):
    hi = x.astype(BF16)
    rest = x - hi.astype(F32)
    mid = rest.astype(BF16)
    lo = (rest - mid.astype(F32)).astype(BF16)
    return hi, mid, lo


def _gate_cum_kernel(gate_ref, b_ref, c_ref):
    r = lax.broadcasted_iota(jnp.int32, (BLOCK, BLOCK), 0)
    c = lax.broadcasted_iota(jnp.int32, (BLOCK, BLOCK), 1)
    lower = (r >= c).astype(BF16)
    is_head = lax.broadcasted_iota(jnp.int32, (1, LANES), 1) < N_FOX_HEADS
    carry = jnp.zeros((1, LANES), F32)
    for blk in range(SEQ // BLOCK):
        rows = slice(blk * BLOCK, (blk + 1) * BLOCK)
        z = gate_ref[rows, :] + b_ref[...]
        log_f = jnp.minimum(z, 0.0) - jnp.log1p(jnp.exp(-jnp.abs(z)))
        cum = sum(_dot(lower, piece) for piece in _split3(log_f)) + carry
        carry = cum[BLOCK - 1:BLOCK, :]
        packed = sum(pltpu.roll(jnp.where(is_head, piece.astype(F32), 0.0), N_FOX_HEADS * i, axis=1)
                     if i else jnp.where(is_head, piece.astype(F32), 0.0)
                     for i, piece in enumerate(_split3(cum * LOG2E)))
        c_ref[0, rows, :] = packed.astype(BF16)


def _gate_cum(gate, b_forget, B):
    return pl.pallas_call(
        _gate_cum_kernel,
        grid=(B,),
        in_specs=[pl.BlockSpec((SEQ, LANES), lambda b: (b, 0)), _resident((1, LANES))],
        out_specs=pl.BlockSpec((1, SEQ, LANES), lambda b: (b, 0, 0)),
        out_shape=jax.ShapeDtypeStruct((B, SEQ, LANES), BF16),
        **_call_params("gate_cum", ("parallel",)),
    )(gate, b_forget)


def _dilated_kernel(slopes_ref, q_ref, k_ref, v_ref, o_ref, ob_ref, lb_ref):
    pair = pl.program_id(1)
    row = lax.broadcasted_iota(jnp.int32, (2 * BLOCK, 1), 0)
    slope = jnp.where(row < BLOCK, slopes_ref[2 * pair], slopes_ref[2 * pair + 1])
    qi = jnp.where(row < BLOCK, row, row - BLOCK)

    def rows(start, size, dilation):
        if dilation == 1:
            return pl.ds(start, size)
        return pl.ds(start, size, stride=dilation)

    for branch, (window, dilation) in enumerate(DIL_CONFIGS):
        steps = window // dilation
        length = SEQ // dilation
        n_blocks = length // BLOCK

        def bias_table(n_keys, first_key):
            kj = lax.broadcasted_iota(jnp.int32, (1, n_keys), 1) + first_key
            delta = qi - kj
            valid = (delta >= 0) & (delta <= steps)
            dist = (delta * dilation).astype(F32)
            return jnp.where(valid, (-LOG2E) * slope * dist, NEG)

        bias_first = bias_table(BLOCK, 0)
        bias_rest = bias_table(2 * BLOCK, -BLOCK)

        def attend(q_start, k_start, n_keys, bias):
            q2 = q_ref[0, rows(q_start, BLOCK, dilation), :].astype(BF16)
            k2 = k_ref[0, rows(k_start, n_keys, dilation), :].astype(BF16)
            v2 = v_ref[0, rows(k_start, n_keys, dilation), :].astype(BF16)
            s = _dot_nt(_stack_heads(q2), k2) + bias
            m = jnp.max(s, axis=-1, keepdims=True)
            p = jnp.exp2(s - m)
            l = jnp.sum(p, axis=-1, keepdims=True)
            o = _dot(p.astype(BF16), v2) / l
            lse = jnp.broadcast_to(m + jnp.log(l) * LOG2E, (2 * BLOCK, LANES))
            ob_ref[branch, rows(q_start, BLOCK, dilation), :] = _unstack_heads(o)
            lb_ref[branch, rows(q_start, BLOCK, dilation), :] = _unstack_heads(lse)

        for r in range(dilation):
            for n in range(n_blocks):
                q_start = r + n * dilation * BLOCK
                if n == 0:
                    attend(q_start, q_start, BLOCK, bias_first)
                else:
                    attend(q_start, q_start - dilation * BLOCK, 2 * BLOCK, bias_rest)

    def mix(i, carry):
        chunk = pl.ds(pl.multiple_of(i * MIX_CHUNK, MIX_CHUNK), MIX_CHUNK)
        lses = [lb_ref[b, chunk, :] for b in range(len(DIL_CONFIGS))]
        top = functools.reduce(jnp.maximum, lses)
        ws = [jnp.exp2(l - top) for l in lses]
        num = sum(w * ob_ref[b, chunk, :] for b, w in enumerate(ws))
        o_ref[0, chunk, :] = (num / sum(ws)).astype(o_ref.dtype)
        return carry

    lax.fori_loop(0, SEQ // MIX_CHUNK, mix, 0)


def _dilated(slopes, qa, ka, va):
    B = qa.shape[0]
    blk = pl.BlockSpec((1, SEQ, LANES), lambda b, p: (b, 0, p))
    n_br = len(DIL_CONFIGS)
    return pl.pallas_call(
        _dilated_kernel,
        grid=(B, HEAD_PAIRS),
        in_specs=[pl.BlockSpec(memory_space=pltpu.SMEM), blk, blk, blk],
        out_specs=blk,
        out_shape=jax.ShapeDtypeStruct((B, SEQ, DIL_WIDTH), BF16),
        scratch_shapes=[pltpu.VMEM((n_br, SEQ, LANES), F32),
                        pltpu.VMEM((n_br, SEQ, LANES), F32)],
        **_call_params("dilated", ("parallel", "parallel")),
    )(slopes, qa, ka, va)


def _fox_kernel(qt_ref, k_ref, c_ref, vt_ref, o_ref,
                s_ref, p_ref, m_ref, alpha_ref, acc_ref):
    first_head = pl.program_id(1) * FOX_HEADS_PER_STEP
    heads = range(FOX_HEADS_PER_STEP)
    tile = FOX_TILE
    kpos = lax.broadcasted_iota(jnp.int32, (tile, tile), 0)
    qpos = lax.broadcasted_iota(jnp.int32, (tile, tile), 1)
    causal = kpos <= qpos
    chan = lax.broadcasted_iota(jnp.int32, (LANES, tile), 0)

    def minus_c(head):
        hit = functools.reduce(jnp.logical_or,
                               [chan == head + N_FOX_HEADS * i for i in range(3)])
        return jnp.where(hit, -1.0, 0.0).astype(BF16)

    pick_c = [minus_c(first_head + h) for h in heads]
    ones_rows = jnp.ones((ONES_ROWS, tile), BF16)
    own = [chan < HEAD_DIM, chan >= HEAD_DIM]

    def tile_slice(t):
        return slice(t * tile, (t + 1) * tile)

    def query_columns(i):
        rhs = []
        for h in heads:
            qt = qt_ref[(h // 2) * LANES:(h // 2 + 1) * LANES, tile_slice(i)]
            rhs.append(jnp.concatenate(
                [jnp.where(own[h % 2], qt, jnp.zeros_like(qt)), pick_c[h]], axis=0))
        return rhs

    def store_scores(rhs, j, h):
        lhs = jnp.concatenate([k_ref[0, tile_slice(j), (h // 2) * LANES:(h // 2 + 1) * LANES],
                               c_ref[0, tile_slice(j), :]], axis=1)
        s_ref[h] = _dot(lhs, rhs[h])

    def accumulate(j, h):
        vt = vt_ref[h * HEAD_DIM:(h + 1) * HEAD_DIM, tile_slice(j)]
        vt_ones = jnp.concatenate([vt, ones_rows], axis=0)
        acc_ref[h] = alpha_ref[h] * acc_ref[h] + _dot(vt_ones, p_ref[h])

    def softmax_step(diagonal, h):
        s = jnp.where(causal, s_ref[h], NEG) if diagonal else s_ref[h]
        m = m_ref[h]
        m_new = jnp.maximum(m, jnp.max(s, axis=0, keepdims=True))
        m_ref[h] = m_new
        alpha_ref[h] = jnp.exp2(m - m_new)
        p_ref[h] = jnp.exp2(s - m_new).astype(BF16)

    def step(j_acc, diagonal, rhs, j_scores):
        for h in heads:
            if j_acc is not None:
                accumulate(j_acc, h)
            softmax_step(diagonal, h)
            if rhs is not None:
                store_scores(rhs, j_scores, h)

    def new_query_tile():
        for h in heads:
            m_ref[h] = jnp.full((1, tile), NEG, F32)

    def write_output(i):
        o_t = jnp.concatenate(
            [acc_ref[h, :HEAD_DIM, :] / acc_ref[h, HEAD_DIM:HEAD_DIM + 1, :] for h in heads],
            axis=0)
        o_ref[0, tile_slice(i), :] = o_t.T.astype(o_ref.dtype)

    n_tiles = SEQ // tile
    rhs0 = query_columns(0)
    for h in heads:
        acc_ref[h] = jnp.zeros((HEAD_DIM + ONES_ROWS, tile), F32)
        store_scores(rhs0, 0, h)
    new_query_tile()
    step(None, True, query_columns(1), 0)

    always = pl.program_id(0) >= 0
    for i in range(1, n_tiles):
        @pl.when(always)
        def _(i=i):
            rhs = query_columns(i)
            new_query_tile()
            step(i - 1, False, rhs, 1)
            write_output(i - 1)
            for j in range(1, i):
                step(j - 1, False, rhs, j + 1)
            step(i - 1, True, query_columns(i + 1) if i + 1 < n_tiles else None, 0)

    for h in heads:
        accumulate(n_tiles - 1, h)
    write_output(n_tiles - 1)


def _fox(qf_t, kf, c_pieces, vf_t):
    B = kf.shape[0]
    width = FOX_HEADS_PER_STEP * HEAD_DIM
    rows_blk = pl.BlockSpec((1, SEQ, width), lambda b, g: (b, 0, g))
    cols_blk = pl.BlockSpec((width, SEQ), lambda b, g: (g, b))
    return pl.pallas_call(
        _fox_kernel,
        grid=(B, N_FOX_HEADS // FOX_HEADS_PER_STEP),
        in_specs=[cols_blk, rows_blk, pl.BlockSpec((1, SEQ, LANES), lambda b, g: (b, 0, 0)),
                  cols_blk],
        out_specs=rows_blk,
        out_shape=jax.ShapeDtypeStruct((B, SEQ, FOX_WIDTH), BF16),
        scratch_shapes=[pltpu.VMEM((FOX_HEADS_PER_STEP, FOX_TILE, FOX_TILE), F32),
                        pltpu.VMEM((FOX_HEADS_PER_STEP, FOX_TILE, FOX_TILE), BF16),
                        pltpu.VMEM((FOX_HEADS_PER_STEP, 1, FOX_TILE), F32),
                        pltpu.VMEM((FOX_HEADS_PER_STEP, 1, FOX_TILE), F32),
                        pltpu.VMEM((FOX_HEADS_PER_STEP, HEAD_DIM + ONES_ROWS, FOX_TILE), F32)],
        **_call_params("fox", ("parallel", "parallel")),
    )(qf_t, kf, c_pieces, vf_t)


def _post_mix_kernel(x_ref, ya_ref, yf_ref, km_ref, vm_ref, woa_ref, wob_ref, g_ref,
                     wq_ref, wo_ref, o_ref):
    x = x_ref[0] + _dot(ya_ref[0], woa_ref[...]) + _dot(yf_ref[0], wob_ref[...])
    h = _rmsnorm(x, g_ref[...]).astype(BF16)
    scale = LOG2E / math.sqrt(XATTN_HEAD_DIM)
    q = (_dot(h, wq_ref[...]) * scale).astype(BF16)
    heads = []
    for hd in range(N_XATTN_HEADS):
        cols = slice(hd * XATTN_HEAD_DIM, (hd + 1) * XATTN_HEAD_DIM)
        s = _dot_nt(q[:, cols], km_ref[0, :, cols])
        p = jnp.exp2(s - jnp.max(s, axis=-1, keepdims=True))
        l = jnp.sum(p, axis=-1, keepdims=True)
        heads.append((_dot(p.astype(BF16), vm_ref[0, :, cols]) / l).astype(BF16))
    o_ref[0] = x + _dot(jnp.concatenate(heads, axis=-1), wo_ref[...])


def _post_mix(x, ya, yf, km, vm, w_oa, w_ob, g_xattn, w_xq, w_xo):
    B = x.shape[0]
    tok = lambda b, i: (b, i, 0)
    per_b = lambda b, i: (b, 0, 0)
    return pl.pallas_call(
        _post_mix_kernel,
        grid=(B, SEQ // ROW_TILE),
        in_specs=[pl.BlockSpec((1, ROW_TILE, D_MODEL), tok),
                  pl.BlockSpec((1, ROW_TILE, DIL_WIDTH), tok),
                  pl.BlockSpec((1, ROW_TILE, FOX_WIDTH), tok),
                  pl.BlockSpec((1, N_MEM, D_MODEL), per_b),
                  pl.BlockSpec((1, N_MEM, D_MODEL), per_b),
                  _resident((DIL_WIDTH, D_MODEL)), _resident((FOX_WIDTH, D_MODEL)),
                  _resident((1, D_MODEL)), _resident((D_MODEL, D_MODEL)),
                  _resident((D_MODEL, D_MODEL))],
        out_specs=pl.BlockSpec((1, ROW_TILE, D_MODEL), tok),
        out_shape=jax.ShapeDtypeStruct(x.shape, F32),
        **_call_params("post_mix", ("parallel", "parallel")),
    )(x, ya, yf, km, vm, w_oa, w_ob, g_xattn, w_xq, w_xo)


def _mlp_kernel(x_ref, g_ref, wu_ref, wd_ref, gf_ref, o_ref):
    x = x_ref[...]
    h = _rmsnorm(x, g_ref[...]).astype(BF16)
    n_chunks = D_FF // FF_CHUNK

    def cols(c):
        return slice(c * FF_CHUNK, (c + 1) * FF_CHUNK)

    def up(c):
        return jnp.maximum(_dot(h, wu_ref[:, cols(c)]), 0.0)

    y = x
    a = up(0)
    for c in range(n_chunks):
        a_next = up(c + 1) if c + 1 < n_chunks else None
        y = y + _dot((a * a).astype(BF16), wd_ref[cols(c), :])
        a = a_next
    o_ref[...] = _rmsnorm(y, gf_ref[...])


def _mlp(x2d, g_mlp, w_up, w_down, g_final):
    T = x2d.shape[0]
    row = pl.BlockSpec((ROW_TILE, D_MODEL), lambda i: (i, 0))
    return pl.pallas_call(
        _mlp_kernel,
        grid=(T // ROW_TILE,),
        in_specs=[row, _resident((1, D_MODEL)), _resident((D_MODEL, D_FF)),
                  _resident((D_FF, D_MODEL)), _resident((1, D_MODEL))],
        out_specs=row,
        out_shape=jax.ShapeDtypeStruct(x2d.shape, F32),
        **_call_params("mlp", ("parallel",)),
    )(x2d, g_mlp, w_up, w_down, g_final)


def kernel(x, mem, g_mix, w_in, b_forget, w_out, g_xattn, g_mem, w_xq, w_xk, w_xv, w_xo,
           g_mlp, w_up, w_down, g_final):
    B, S, D = x.shape
    assert (S, D) == (SEQ, D_MODEL) and mem.shape == (B, N_MEM, D_MODEL)
    gain = lambda g: g.reshape(1, D_MODEL).astype(F32)
    bf = lambda w: w.astype(BF16)
    slopes = 2.0 ** (-(jnp.arange(1, N_DIL_HEADS + 1, dtype=F32) * (8.0 / N_DIL_HEADS)))

    km, vm = _mem_kv(mem, gain(g_mem), bf(w_xk), bf(w_xv))
    o_qf = 3 * DIL_WIDTH
    o_kf, o_vf = o_qf + FOX_WIDTH, o_qf + 2 * FOX_WIDTH
    pad_heads = lambda a: jnp.pad(a, ((0, 0), (0, LANES - N_FOX_HEADS)))
    qa, ka, va, qf_t, kf, vf_t, gate = _in_proj(
        x.reshape(B * S, D), gain(g_mix), bf(w_in[:, :o_qf]), bf(w_in[:, o_kf:o_vf]),
        bf(w_in[:, o_qf:o_kf].T), bf(w_in[:, o_vf:QKV_WIDTH].T), bf(pad_heads(w_in[:, QKV_WIDTH:])))
    c_pieces = _gate_cum(gate, pad_heads(b_forget.reshape(1, N_FOX_HEADS).astype(F32)), B)
    seq = lambda a: a.reshape(B, S, a.shape[-1])
    ya = _dilated(slopes, seq(qa), seq(ka), seq(va))
    yf = _fox(qf_t, seq(kf), c_pieces, vf_t)
    x = _post_mix(x, ya, yf, km, vm, bf(w_out[:DIL_WIDTH]), bf(w_out[DIL_WIDTH:]),
                  gain(g_xattn), bf(w_xq), bf(w_xo))
    y = _mlp(x.reshape(B * S, D), gain(g_mlp), bf(w_up), bf(w_down), gain(g_final))
    return y.reshape(B, S, D)
```

```python
import functools
import math

import jax
import jax.numpy as jnp
from jax import lax
from jax.experimental import pallas as pl
from jax.experimental.pallas import tpu as pltpu

D_MODEL = 1024
SEQ = 2048
N_MEM = 256
HEAD_DIM = 64
N_DIL_HEADS = 8
N_FOX_HEADS = 8
DIL_WIDTH = N_DIL_HEADS * HEAD_DIM
FOX_WIDTH = N_FOX_HEADS * HEAD_DIM
QKV_WIDTH = 3 * DIL_WIDTH + 3 * FOX_WIDTH
DIL_CONFIGS = ((128, 1), (512, 4), (2048, 16))
BLOCK = 128
N_XATTN_HEADS = 4
XATTN_HEAD_DIM = D_MODEL // N_XATTN_HEADS
D_FF = 4 * D_MODEL
EPS = 1e-6
NEG = -1e30
LOG2E = math.log2(math.e)

LANES = 128
HEAD_PAIRS = N_FOX_HEADS // 2
ROW_TILE = 1024
FOX_TILE = 256
FOX_HEADS_PER_STEP = 8
ONES_ROWS = 16
FF_CHUNK = 1024
MIX_CHUNK = 256

F32 = jnp.float32
BF16 = jnp.bfloat16

_NT = (((1,), (1,)), ((), ()))


def _dot(a, b):
    return jnp.dot(a, b, preferred_element_type=F32)


def _dot_nt(a, b):
    return lax.dot_general(a, b, _NT, preferred_element_type=F32)


def _rmsnorm(x, g):
    return x * lax.rsqrt(jnp.mean(x * x, axis=-1, keepdims=True) + EPS) * g


def _resident(shape):
    zeros = (0,) * len(shape)
    return pl.BlockSpec(shape, lambda *_: zeros, pipeline_mode=pl.Buffered(1))


VMEM_MIB = {"mem_kv": 24, "in_proj": 52, "gate_cum": 16, "dilated": 32, "fox": 32,
            "post_mix": 56, "mlp": 58}


def _call_params(name, semantics):
    return dict(name=name, compiler_params=pltpu.CompilerParams(
        dimension_semantics=semantics, vmem_limit_bytes=VMEM_MIB[name] * 1024 * 1024))


def _stack_heads(q2):
    lane = lax.broadcasted_iota(jnp.int32, (1, LANES), 1)
    zero = jnp.zeros_like(q2)
    return jnp.concatenate([jnp.where(lane < HEAD_DIM, q2, zero),
                            jnp.where(lane >= HEAD_DIM, q2, zero)], axis=0)


def _unstack_heads(o):
    rows = o.shape[0] // 2
    lane = lax.broadcasted_iota(jnp.int32, (1, LANES), 1)
    return jnp.where(lane < HEAD_DIM, o[:rows], o[rows:])


def _mem_kv_kernel(mem_ref, g_ref, wk_ref, wv_ref, k_ref, v_ref):
    m = _rmsnorm(mem_ref[0], g_ref[...]).astype(BF16)
    k_ref[0] = _dot(m, wk_ref[...]).astype(BF16)
    v_ref[0] = _dot(m, wv_ref[...]).astype(BF16)


def _mem_kv(mem, g_mem, w_xk, w_xv):
    B = mem.shape[0]
    blk = pl.BlockSpec((1, N_MEM, D_MODEL), lambda b: (b, 0, 0))
    return pl.pallas_call(
        _mem_kv_kernel,
        grid=(B,),
        in_specs=[blk, _resident((1, D_MODEL)), _resident((D_MODEL, D_MODEL)),
                  _resident((D_MODEL, D_MODEL))],
        out_specs=[blk, blk],
        out_shape=[jax.ShapeDtypeStruct((B, N_MEM, D_MODEL), BF16)] * 2,
        **_call_params("mem_kv", ("parallel",)),
    )(mem, g_mem, w_xk, w_xv)


def _in_proj_kernel(x_ref, g_ref, wa_ref, wk_ref, wqt_ref, wvt_ref, wg_ref,
                    qa, ka, va, qft, kf, vft, gate):
    h = _rmsnorm(x_ref[...], g_ref[...]).astype(BF16)
    scale = LOG2E / math.sqrt(HEAD_DIM)
    qa[...] = _dot(h, wa_ref[:, :DIL_WIDTH]) * scale
    ka[...] = _dot(h, wa_ref[:, DIL_WIDTH:2 * DIL_WIDTH])
    va[...] = _dot(h, wa_ref[:, 2 * DIL_WIDTH:])
    qft[...] = (_dot_nt(wqt_ref[...], h) * scale).astype(BF16)
    kf[...] = _dot(h, wk_ref[...]).astype(BF16)
    vft[...] = _dot_nt(wvt_ref[...], h).astype(BF16)
    gate[...] = _dot(h, wg_ref[...])


def _in_proj(x2d, g_mix, w_a, w_kf, w_qf_t, w_vf_t, w_gate):
    T = x2d.shape[0]
    row = lambda i: (i, 0)
    col = lambda i: (0, i)
    rows_blk = pl.BlockSpec((ROW_TILE, DIL_WIDTH), row)
    cols_blk = pl.BlockSpec((FOX_WIDTH, ROW_TILE), col)
    return pl.pallas_call(
        _in_proj_kernel,
        grid=(T // ROW_TILE,),
        in_specs=[pl.BlockSpec((ROW_TILE, D_MODEL), row), _resident((1, D_MODEL)),
                  _resident((D_MODEL, 3 * DIL_WIDTH)), _resident((D_MODEL, FOX_WIDTH)),
                  _resident((FOX_WIDTH, D_MODEL)), _resident((FOX_WIDTH, D_MODEL)),
                  _resident((D_MODEL, LANES))],
        out_specs=[rows_blk] * 3 + [cols_blk, rows_blk, cols_blk,
                                    pl.BlockSpec((ROW_TILE, LANES), row)],
        out_shape=[jax.ShapeDtypeStruct((T, DIL_WIDTH), F32)] * 3
        + [jax.ShapeDtypeStruct((FOX_WIDTH, T), BF16), jax.ShapeDtypeStruct((T, FOX_WIDTH), BF16),
           jax.ShapeDtypeStruct((FOX_WIDTH, T), BF16), jax.ShapeDtypeStruct((T, LANES), F32)],
        **_call_params("in_proj", ("parallel",)),
    )(x2d, g_mix, w_a, w_kf, w_qf_t, w_vf_t, w_gate)


def _split3(x):
    hi = x.astype(BF16)
    rest = x - hi.astype(F32)
    mid = rest.astype(BF16)
    lo = (rest - mid.astype(F32)).astype(BF16)
    return hi, mid, lo


def _gate_cum_kernel(gate_ref, b_ref, c_ref):
    r = lax.broadcasted_iota(jnp.int32, (BLOCK, BLOCK), 0)
    c = lax.broadcasted_iota(jnp.int32, (BLOCK, BLOCK), 1)
    lower = (r >= c).astype(BF16)
    is_head = lax.broadcasted_iota(jnp.int32, (1, LANES), 1) < N_FOX_HEADS
    carry = jnp.zeros((1, LANES), F32)
    for blk in range(SEQ // BLOCK):
        rows = slice(blk * BLOCK, (blk + 1) * BLOCK)
        z = gate_ref[rows, :] + b_ref[...]
        log_f = jnp.minimum(z, 0.0) - jnp.log1p(jnp.exp(-jnp.abs(z)))
        cum = sum(_dot(lower, piece) for piece in _split3(log_f)) + carry
        carry = cum[BLOCK - 1:BLOCK, :]
        packed = sum(pltpu.roll(jnp.where(is_head, piece.astype(F32), 0.0), N_FOX_HEADS * i, axis=1)
                     if i else jnp.where(is_head, piece.astype(F32), 0.0)
                     for i, piece in enumerate(_split3(cum * LOG2E)))
        c_ref[0, rows, :] = packed.astype(BF16)


def _gate_cum(gate, b_forget, B):
    return pl.pallas_call(
        _gate_cum_kernel,
        grid=(B,),
        in_specs=[pl.BlockSpec((SEQ, LANES), lambda b: (b, 0)), _resident((1, LANES))],
        out_specs=pl.BlockSpec((1, SEQ, LANES), lambda b: (b, 0, 0)),
        out_shape=jax.ShapeDtypeStruct((B, SEQ, LANES), BF16),
        **_call_params("gate_cum", ("parallel",)),
    )(gate, b_forget)


def _dilated_kernel(slopes_ref, q_ref, k_ref, v_ref, o_ref, ob_ref, lb_ref):
    pair = pl.program_id(1)
    row = lax.broadcasted_iota(jnp.int32, (2 * BLOCK, 1), 0)
    slope = jnp.where(row < BLOCK, slopes_ref[2 * pair], slopes_ref[2 * pair + 1])
    qi = jnp.where(row < BLOCK, row, row - BLOCK)

    def rows(start, size, dilation):
        if dilation == 1:
            return pl.ds(start, size)
        return pl.ds(start, size, stride=dilation)

    for branch, (window, dilation) in enumerate(DIL_CONFIGS):
        steps = window // dilation
        length = SEQ // dilation
        n_blocks = length // BLOCK

        def bias_table(n_keys, first_key):
            kj = lax.broadcasted_iota(jnp.int32, (1, n_keys), 1) + first_key
            delta = qi - kj
            valid = (delta >= 0) & (delta <= steps)
            dist = (delta * dilation).astype(F32)
            return jnp.where(valid, (-LOG2E) * slope * dist, NEG)

        bias_first = bias_table(BLOCK, 0)
        bias_rest = bias_table(2 * BLOCK, -BLOCK)

        def attend(q_start, k_start, n_keys, bias):
            q2 = q_ref[0, rows(q_start, BLOCK, dilation), :].astype(BF16)
            k2 = k_ref[0, rows(k_start, n_keys, dilation), :].astype(BF16)
            v2 = v_ref[0, rows(k_start, n_keys, dilation), :].astype(BF16)
            s = _dot_nt(_stack_heads(q2), k2) + bias
            m = jnp.max(s, axis=-1, keepdims=True)
            p = jnp.exp2(s - m)
            l = jnp.sum(p, axis=-1, keepdims=True)
            o = _dot(p.astype(BF16), v2) / l
            lse = jnp.broadcast_to(m + jnp.log(l) * LOG2E, (2 * BLOCK, LANES))
            ob_ref[branch, rows(q_start, BLOCK, dilation), :] = _unstack_heads(o)
            lb_ref[branch, rows(q_start, BLOCK, dilation), :] = _unstack_heads(lse)

        for r in range(dilation):
            for n in range(n_blocks):
                q_start = r + n * dilation * BLOCK
                if n == 0:
                    attend(q_start, q_start, BLOCK, bias_first)
                else:
                    attend(q_start, q_start - dilation * BLOCK, 2 * BLOCK, bias_rest)

    def mix(i, carry):
        chunk = pl.ds(pl.multiple_of(i * MIX_CHUNK, MIX_CHUNK), MIX_CHUNK)
        lses = [lb_ref[b, chunk, :] for b in range(len(DIL_CONFIGS))]
        top = functools.reduce(jnp.maximum, lses)
        ws = [jnp.exp2(l - top) for l in lses]
        num = sum(w * ob_ref[b, chunk, :] for b, w in enumerate(ws))
        o_ref[0, chunk, :] = (num / sum(ws)).astype(o_ref.dtype)
        return carry

    lax.fori_loop(0, SEQ // MIX_CHUNK, mix, 0)


def _dilated(slopes, qa, ka, va):
    B = qa.shape[0]
    blk = pl.BlockSpec((1, SEQ, LANES), lambda b, p: (b, 0, p))
    n_br = len(DIL_CONFIGS)
    return pl.pallas_call(
        _dilated_kernel,
        grid=(B, HEAD_PAIRS),
        in_specs=[pl.BlockSpec(memory_space=pltpu.SMEM), blk, blk, blk],
        out_specs=blk,
        out_shape=jax.ShapeDtypeStruct((B, SEQ, DIL_WIDTH), BF16),
        scratch_shapes=[pltpu.VMEM((n_br, SEQ, LANES), F32),
                        pltpu.VMEM((n_br, SEQ, LANES), F32)],
        **_call_params("dilated", ("parallel", "parallel")),
    )(slopes, qa, ka, va)


def _fox_kernel(qt_ref, k_ref, c_ref, vt_ref, o_ref,
                s_ref, p_ref, m_ref, alpha_ref, acc_ref):
    first_head = pl.program_id(1) * FOX_HEADS_PER_STEP
    heads = range(FOX_HEADS_PER_STEP)
    tile = FOX_TILE
    kpos = lax.broadcasted_iota(jnp.int32, (tile, tile), 0)
    qpos = lax.broadcasted_iota(jnp.int32, (tile, tile), 1)
    causal = kpos <= qpos
    chan = lax.broadcasted_iota(jnp.int32, (LANES, tile), 0)

    def minus_c(head):
        hit = functools.reduce(jnp.logical_or,
                               [chan == head + N_FOX_HEADS * i for i in range(3)])
        return jnp.where(hit, -1.0, 0.0).astype(BF16)

    pick_c = [minus_c(first_head + h) for h in heads]
    ones_rows = jnp.ones((ONES_ROWS, tile), BF16)
    own = [chan < HEAD_DIM, chan >= HEAD_DIM]

    def tile_slice(t):
        return slice(t * tile, (t + 1) * tile)

    def query_columns(i):
        rhs = []
        for h in heads:
            qt = qt_ref[(h // 2) * LANES:(h // 2 + 1) * LANES, tile_slice(i)]
            rhs.append(jnp.concatenate(
                [jnp.where(own[h % 2], qt, jnp.zeros_like(qt)), pick_c[h]], axis=0))
        return rhs

    def store_scores(rhs, j, h):
        lhs = jnp.concatenate([k_ref[0, tile_slice(j), (h // 2) * LANES:(h // 2 + 1) * LANES],
                               c_ref[0, tile_slice(j), :]], axis=1)
        s_ref[h] = _dot(lhs, rhs[h])

    def accumulate(j, h):
        vt = vt_ref[h * HEAD_DIM:(h + 1) * HEAD_DIM, tile_slice(j)]
        vt_ones = jnp.concatenate([vt, ones_rows], axis=0)
        acc_ref[h] = alpha_ref[h] * acc_ref[h] + _dot(vt_ones, p_ref[h])

    def softmax_step(diagonal, h):
        s = jnp.where(causal, s_ref[h], NEG) if diagonal else s_ref[h]
        m = m_ref[h]
        m_new = jnp.maximum(m, jnp.max(s, axis=0, keepdims=True))
        m_ref[h] = m_new
        alpha_ref[h] = jnp.exp2(m - m_new)
        p_ref[h] = jnp.exp2(s - m_new).astype(BF16)

    def step(j_acc, diagonal, rhs, j_scores):
        for h in heads:
            if j_acc is not None:
                accumulate(j_acc, h)
            softmax_step(diagonal, h)
            if rhs is not None:
                store_scores(rhs, j_scores, h)

    def new_query_tile():
        for h in heads:
            m_ref[h] = jnp.full((1, tile), NEG, F32)

    def write_output(i):
        o_t = jnp.concatenate(
            [acc_ref[h, :HEAD_DIM, :] / acc_ref[h, HEAD_DIM:HEAD_DIM + 1, :] for h in heads],
            axis=0)
        o_ref[0, tile_slice(i), :] = o_t.T.astype(o_ref.dtype)

    n_tiles = SEQ // tile
    rhs0 = query_columns(0)
    for h in heads:
        acc_ref[h] = jnp.zeros((HEAD_DIM + ONES_ROWS, tile), F32)
        store_scores(rhs0, 0, h)
    new_query_tile()
    step(None, True, query_columns(1), 0)

    always = pl.program_id(0) >= 0
    for i in range(1, n_tiles):
        @pl.when(always)
        def _(i=i):
            rhs = query_columns(i)
            new_query_tile()
            step(i - 1, False, rhs, 1)
            write_output(i - 1)
            for j in range(1, i):
                step(j - 1, False, rhs, j + 1)
            step(i - 1, True, query_columns(i + 1) if i + 1 < n_tiles else None, 0)

    for h in heads:
        accumulate(n_tiles - 1, h)
    write_output(n_tiles - 1)


def _fox(qf_t, kf, c_pieces, vf_t):
    B = kf.shape[0]
    width = FOX_HEADS_PER_STEP * HEAD_DIM
    rows_blk = pl.BlockSpec((1, SEQ, width), lambda b, g: (b, 0, g))
    cols_blk = pl.BlockSpec((width, SEQ), lambda b, g: (g, b))
    return pl.pallas_call(
        _fox_kernel,
        grid=(B, N_FOX_HEADS // FOX_HEADS_PER_STEP),
        in_specs=[cols_blk, rows_blk, pl.BlockSpec((1, SEQ, LANES), lambda b, g: (b, 0, 0)),
                  cols_blk],
        out_specs=rows_blk,
        out_shape=jax.ShapeDtypeStruct((B, SEQ, FOX_WIDTH), BF16),
        scratch_shapes=[pltpu.VMEM((FOX_HEADS_PER_STEP, FOX_TILE, FOX_TILE), F32),
                        pltpu.VMEM((FOX_HEADS_PER_STEP, FOX_TILE, FOX_TILE), BF16),
                        pltpu.VMEM((FOX_HEADS_PER_STEP, 1, FOX_TILE), F32),
                        pltpu.VMEM((FOX_HEADS_PER_STEP, 1, FOX_TILE), F32),
                        pltpu.VMEM((FOX_HEADS_PER_STEP, HEAD_DIM + ONES_ROWS, FOX_TILE), F32)],
        **_call_params("fox", ("parallel", "parallel")),
    )(qf_t, kf, c_pieces, vf_t)


def _post_mix_kernel(x_ref, ya_ref, yf_ref, km_ref, vm_ref, woa_ref, wob_ref, g_ref,
                     wq_ref, wo_ref, o_ref):
    x = x_ref[0] + _dot(ya_ref[0], woa_ref[...]) + _dot(yf_ref[0], wob_ref[...])
    h = _rmsnorm(x, g_ref[...]).astype(BF16)
    scale = LOG2E / math.sqrt(XATTN_HEAD_DIM)
    q = (_dot(h, wq_ref[...]) * scale).astype(BF16)
    def cols(hd):
        return slice(hd * XATTN_HEAD_DIM, (hd + 1) * XATTN_HEAD_DIM)

    def scores(hd):
        return _dot_nt(q[:, cols(hd)], km_ref[0, :, cols(hd)])

    heads = []
    s = scores(0)
    for hd in range(N_XATTN_HEADS):
        s_next = scores(hd + 1) if hd + 1 < N_XATTN_HEADS else None
        p = jnp.exp2(s - jnp.max(s, axis=-1, keepdims=True))
        l = jnp.sum(p, axis=-1, keepdims=True)
        heads.append((_dot(p.astype(BF16), vm_ref[0, :, cols(hd)]) / l).astype(BF16))
        s = s_next
    o_ref[0] = x + _dot(jnp.concatenate(heads, axis=-1), wo_ref[...])


def _post_mix(x, ya, yf, km, vm, w_oa, w_ob, g_xattn, w_xq, w_xo):
    B = x.shape[0]
    tok = lambda b, i: (b, i, 0)
    per_b = lambda b, i: (b, 0, 0)
    return pl.pallas_call(
        _post_mix_kernel,
        grid=(B, SEQ // ROW_TILE),
        in_specs=[pl.BlockSpec((1, ROW_TILE, D_MODEL), tok),
                  pl.BlockSpec((1, ROW_TILE, DIL_WIDTH), tok),
                  pl.BlockSpec((1, ROW_TILE, FOX_WIDTH), tok),
                  pl.BlockSpec((1, N_MEM, D_MODEL), per_b),
                  pl.BlockSpec((1, N_MEM, D_MODEL), per_b),
                  _resident((DIL_WIDTH, D_MODEL)), _resident((FOX_WIDTH, D_MODEL)),
                  _resident((1, D_MODEL)), _resident((D_MODEL, D_MODEL)),
                  _resident((D_MODEL, D_MODEL))],
        out_specs=pl.BlockSpec((1, ROW_TILE, D_MODEL), tok),
        out_shape=jax.ShapeDtypeStruct(x.shape, F32),
        **_call_params("post_mix", ("parallel", "parallel")),
    )(x, ya, yf, km, vm, w_oa, w_ob, g_xattn, w_xq, w_xo)


def _mlp_kernel(x_ref, g_ref, wu_ref, wd_ref, gf_ref, o_ref):
    x = x_ref[...]
    h = _rmsnorm(x, g_ref[...]).astype(BF16)
    y = x
    for c in range(D_FF // FF_CHUNK):
        cols = slice(c * FF_CHUNK, (c + 1) * FF_CHUNK)
        a = jnp.maximum(_dot(h, wu_ref[:, cols]), 0.0)
        y = y + _dot((a * a).astype(BF16), wd_ref[cols, :])
    o_ref[...] = _rmsnorm(y, gf_ref[...])


def _mlp(x2d, g_mlp, w_up, w_down, g_final):
    T = x2d.shape[0]
    row = pl.BlockSpec((ROW_TILE, D_MODEL), lambda i: (i, 0))
    return pl.pallas_call(
        _mlp_kernel,
        grid=(T // ROW_TILE,),
        in_specs=[row, _resident((1, D_MODEL)), _resident((D_MODEL, D_FF)),
                  _resident((D_FF, D_MODEL)), _resident((1, D_MODEL))],
        out_specs=row,
        out_shape=jax.ShapeDtypeStruct(x2d.shape, F32),
        **_call_params("mlp", ("parallel",)),
    )(x2d, g_mlp, w_up, w_down, g_final)


def kernel(x, mem, g_mix, w_in, b_forget, w_out, g_xattn, g_mem, w_xq, w_xk, w_xv, w_xo,
           g_mlp, w_up, w_down, g_final):
    B, S, D = x.shape
    assert (S, D) == (SEQ, D_MODEL) and mem.shape == (B, N_MEM, D_MODEL)
    gain = lambda g: g.reshape(1, D_MODEL).astype(F32)
    bf = lambda w: w.astype(BF16)
    slopes = 2.0 ** (-(jnp.arange(1, N_DIL_HEADS + 1, dtype=F32) * (8.0 / N_DIL_HEADS)))

    km, vm = _mem_kv(mem, gain(g_mem), bf(w_xk), bf(w_xv))
    o_qf = 3 * DIL_WIDTH
    o_kf, o_vf = o_qf + FOX_WIDTH, o_qf + 2 * FOX_WIDTH
    pad_heads = lambda a: jnp.pad(a, ((0, 0), (0, LANES - N_FOX_HEADS)))
    qa, ka, va, qf_t, kf, vf_t, gate = _in_proj(
        x.reshape(B * S, D), gain(g_mix), bf(w_in[:, :o_qf]), bf(w_in[:, o_kf:o_vf]),
        bf(w_in[:, o_qf:o_kf].T), bf(w_in[:, o_vf:QKV_WIDTH].T), bf(pad_heads(w_in[:, QKV_WIDTH:])))
    c_pieces = _gate_cum(gate, pad_heads(b_forget.reshape(1, N_FOX_HEADS).astype(F32)), B)
    seq = lambda a: a.reshape(B, S, a.shape[-1])
    ya = _dilated(slopes, seq(qa), seq(ka), seq(va))
    yf = _fox(qf_t, seq(kf), c_pieces, vf_t)
    x = _post_mix(x, ya, yf, km, vm, bf(w_out[:DIL_WIDTH]), bf(w_out[DIL_WIDTH:]),
                  gain(g_xattn), bf(w_xq), bf(w_xo))
    y = _mlp(x.reshape(B * S, D), gain(g_mlp), bf(w_up), bf(w_down), gain(g_final))
    return y.reshape(B, S, D)
```

```python
import functools
import math

import jax
import jax.numpy as jnp
from jax import lax
from jax.experimental import pallas as pl
from jax.experimental.pallas import tpu as pltpu

D_MODEL = 1024
SEQ = 2048
N_MEM = 256
HEAD_DIM = 64
N_DIL_HEADS = 8
N_FOX_HEADS = 8
DIL_WIDTH = N_DIL_HEADS * HEAD_DIM
FOX_WIDTH = N_FOX_HEADS * HEAD_DIM
QKV_WIDTH = 3 * DIL_WIDTH + 3 * FOX_WIDTH
DIL_CONFIGS = ((128, 1), (512, 4), (2048, 16))
BLOCK = 128
N_XATTN_HEADS = 4
XATTN_HEAD_DIM = D_MODEL // N_XATTN_HEADS
D_FF = 4 * D_MODEL
EPS = 1e-6
NEG = -1e30
LOG2E = math.log2(math.e)

LANES = 128
HEAD_PAIRS = N_FOX_HEADS // 2
ROW_TILE = 1024
FOX_TILE = 256
FOX_HEADS_PER_STEP = 8
ONES_ROWS = 16
FF_CHUNK = 1024
MIX_CHUNK = 256

F32 = jnp.float32
BF16 = jnp.bfloat16

_NT = (((1,), (1,)), ((), ()))


def _dot(a, b):
    return jnp.dot(a, b, preferred_element_type=F32)


def _dot_nt(a, b):
    return lax.dot_general(a, b, _NT, preferred_element_type=F32)


def _rmsnorm(x, g):
    return x * lax.rsqrt(jnp.mean(x * x, axis=-1, keepdims=True) + EPS) * g


def _resident(shape):
    zeros = (0,) * len(shape)
    return pl.BlockSpec(shape, lambda *_: zeros, pipeline_mode=pl.Buffered(1))


VMEM_MIB = {"mem_kv": 24, "in_proj": 52, "gate_cum": 16, "dilated": 32, "fox": 32,
            "post_mix": 56, "mlp": 58}


def _call_params(name, semantics):
    return dict(name=name, compiler_params=pltpu.CompilerParams(
        dimension_semantics=semantics, vmem_limit_bytes=VMEM_MIB[name] * 1024 * 1024))


def _stack_heads(q2):
    lane = lax.broadcasted_iota(jnp.int32, (1, LANES), 1)
    zero = jnp.zeros_like(q2)
    return jnp.concatenate([jnp.where(lane < HEAD_DIM, q2, zero),
                            jnp.where(lane >= HEAD_DIM, q2, zero)], axis=0)


def _unstack_heads(o):
    rows = o.shape[0] // 2
    lane = lax.broadcasted_iota(jnp.int32, (1, LANES), 1)
    return jnp.where(lane < HEAD_DIM, o[:rows], o[rows:])


def _mem_kv_kernel(mem_ref, g_ref, wk_ref, wv_ref, k_ref, v_ref):
    m = _rmsnorm(mem_ref[0], g_ref[...]).astype(BF16)
    k_ref[0] = _dot(m, wk_ref[...]).astype(BF16)
    v_ref[0] = _dot(m, wv_ref[...]).astype(BF16)


def _mem_kv(mem, g_mem, w_xk, w_xv):
    B = mem.shape[0]
    blk = pl.BlockSpec((1, N_MEM, D_MODEL), lambda b: (b, 0, 0))
    return pl.pallas_call(
        _mem_kv_kernel,
        grid=(B,),
        in_specs=[blk, _resident((1, D_MODEL)), _resident((D_MODEL, D_MODEL)),
                  _resident((D_MODEL, D_MODEL))],
        out_specs=[blk, blk],
        out_shape=[jax.ShapeDtypeStruct((B, N_MEM, D_MODEL), BF16)] * 2,
        **_call_params("mem_kv", ("parallel",)),
    )(mem, g_mem, w_xk, w_xv)


def _in_proj_kernel(x_ref, g_ref, wa_ref, wk_ref, wqt_ref, wvt_ref, wg_ref,
                    qa, ka, va, qft, kf, vft, gate):
    h = _rmsnorm(x_ref[...], g_ref[...]).astype(BF16)
    scale = LOG2E / math.sqrt(HEAD_DIM)
    qa[...] = _dot(h, wa_ref[:, :DIL_WIDTH]) * scale
    ka[...] = _dot(h, wa_ref[:, DIL_WIDTH:2 * DIL_WIDTH])
    va[...] = _dot(h, wa_ref[:, 2 * DIL_WIDTH:])
    qft[...] = (_dot_nt(wqt_ref[...], h) * scale).astype(BF16)
    kf[...] = _dot(h, wk_ref[...]).astype(BF16)
    vft[...] = _dot_nt(wvt_ref[...], h).astype(BF16)
    gate[...] = _dot(h, wg_ref[...])


def _in_proj(x2d, g_mix, w_a, w_kf, w_qf_t, w_vf_t, w_gate):
    T = x2d.shape[0]
    row = lambda i: (i, 0)
    col = lambda i: (0, i)
    rows_blk = pl.BlockSpec((ROW_TILE, DIL_WIDTH), row)
    cols_blk = pl.BlockSpec((FOX_WIDTH, ROW_TILE), col)
    return pl.pallas_call(
        _in_proj_kernel,
        grid=(T // ROW_TILE,),
        in_specs=[pl.BlockSpec((ROW_TILE, D_MODEL), row), _resident((1, D_MODEL)),
                  _resident((D_MODEL, 3 * DIL_WIDTH)), _resident((D_MODEL, FOX_WIDTH)),
                  _resident((FOX_WIDTH, D_MODEL)), _resident((FOX_WIDTH, D_MODEL)),
                  _resident((D_MODEL, LANES))],
        out_specs=[rows_blk] * 3 + [cols_blk, rows_blk, cols_blk,
                                    pl.BlockSpec((ROW_TILE, LANES), row)],
        out_shape=[jax.ShapeDtypeStruct((T, DIL_WIDTH), F32)] * 3
        + [jax.ShapeDtypeStruct((FOX_WIDTH, T), BF16), jax.ShapeDtypeStruct((T, FOX_WIDTH), BF16),
           jax.ShapeDtypeStruct((FOX_WIDTH, T), BF16), jax.ShapeDtypeStruct((T, LANES), F32)],
        **_call_params("in_proj", ("parallel",)),
    )(x2d, g_mix, w_a, w_kf, w_qf_t, w_vf_t, w_gate)


def _split3(x):
    hi = x.astype(BF16)
    rest = x - hi.astype(F32)
    mid = rest.astype(BF16)
    lo = (rest - mid.astype(F32)).astype(BF16)
    return hi, mid, lo


def _gate_cum_kernel(gate_ref, b_ref, c_ref):
    r = lax.broadcasted_iota(jnp.int32, (BLOCK, BLOCK), 0)
    c = lax.broadcasted_iota(jnp.int32, (BLOCK, BLOCK), 1)
    lower = (r >= c).astype(BF16)
    is_head = lax.broadcasted_iota(jnp.int32, (1, LANES), 1) < N_FOX_HEADS
    carry = jnp.zeros((1, LANES), F32)
    for blk in range(SEQ // BLOCK):
        rows = slice(blk * BLOCK, (blk + 1) * BLOCK)
        z = gate_ref[rows, :] + b_ref[...]
        log_f = jnp.minimum(z, 0.0) - jnp.log1p(jnp.exp(-jnp.abs(z)))
        cum = sum(_dot(lower, piece) for piece in _split3(log_f)) + carry
        carry = cum[BLOCK - 1:BLOCK, :]
        packed = sum(pltpu.roll(jnp.where(is_head, piece.astype(F32), 0.0), N_FOX_HEADS * i, axis=1)
                     if i else jnp.where(is_head, piece.astype(F32), 0.0)
                     for i, piece in enumerate(_split3(cum * LOG2E)))
        c_ref[0, rows, :] = packed.astype(BF16)


def _gate_cum(gate, b_forget, B):
    return pl.pallas_call(
        _gate_cum_kernel,
        grid=(B,),
        in_specs=[pl.BlockSpec((SEQ, LANES), lambda b: (b, 0)), _resident((1, LANES))],
        out_specs=pl.BlockSpec((1, SEQ, LANES), lambda b: (b, 0, 0)),
        out_shape=jax.ShapeDtypeStruct((B, SEQ, LANES), BF16),
        **_call_params("gate_cum", ("parallel",)),
    )(gate, b_forget)


def _dilated_kernel(slopes_ref, q_ref, k_ref, v_ref, o_ref, ob_ref, lb_ref):
    pair = pl.program_id(1)
    row = lax.broadcasted_iota(jnp.int32, (2 * BLOCK, 1), 0)
    slope = jnp.where(row < BLOCK, slopes_ref[2 * pair], slopes_ref[2 * pair + 1])
    qi = jnp.where(row < BLOCK, row, row - BLOCK)

    def rows(start, size, dilation):
        if dilation == 1:
            return pl.ds(start, size)
        return pl.ds(start, size, stride=dilation)

    for branch, (window, dilation) in enumerate(DIL_CONFIGS):
        steps = window // dilation
        length = SEQ // dilation
        n_blocks = length // BLOCK

        def bias_table(n_keys, first_key):
            kj = lax.broadcasted_iota(jnp.int32, (1, n_keys), 1) + first_key
            delta = qi - kj
            valid = (delta >= 0) & (delta <= steps)
            dist = (delta * dilation).astype(F32)
            return jnp.where(valid, (-LOG2E) * slope * dist, NEG)

        bias_first = bias_table(BLOCK, 0)
        bias_rest = bias_table(2 * BLOCK, -BLOCK)

        def attend(q_start, k_start, n_keys, bias):
            q2 = q_ref[0, rows(q_start, BLOCK, dilation), :].astype(BF16)
            k2 = k_ref[0, rows(k_start, n_keys, dilation), :].astype(BF16)
            v2 = v_ref[0, rows(k_start, n_keys, dilation), :].astype(BF16)
            s = _dot_nt(_stack_heads(q2), k2) + bias
            m = jnp.max(s, axis=-1, keepdims=True)
            p = jnp.exp2(s - m)
            l = jnp.sum(p, axis=-1, keepdims=True)
            o = _dot(p.astype(BF16), v2) / l
            lse = jnp.broadcast_to(m + jnp.log(l) * LOG2E, (2 * BLOCK, LANES))
            ob_ref[branch, rows(q_start, BLOCK, dilation), :] = _unstack_heads(o)
            lb_ref[branch, rows(q_start, BLOCK, dilation), :] = _unstack_heads(lse)

        for r in range(dilation):
            for n in range(n_blocks):
                q_start = r + n * dilation * BLOCK
                if n == 0:
                    attend(q_start, q_start, BLOCK, bias_first)
                else:
                    attend(q_start, q_start - dilation * BLOCK, 2 * BLOCK, bias_rest)

    def mix(i, carry):
        chunk = pl.ds(pl.multiple_of(i * MIX_CHUNK, MIX_CHUNK), MIX_CHUNK)
        lses = [lb_ref[b, chunk, :] for b in range(len(DIL_CONFIGS))]
        top = functools.reduce(jnp.maximum, lses)
        ws = [jnp.exp2(l - top) for l in lses]
        num = sum(w * ob_ref[b, chunk, :] for b, w in enumerate(ws))
        o_ref[0, chunk, :] = (num / sum(ws)).astype(o_ref.dtype)
        return carry

    lax.fori_loop(0, SEQ // MIX_CHUNK, mix, 0)


def _dilated(slopes, qa, ka, va):
    B = qa.shape[0]
    blk = pl.BlockSpec((1, SEQ, LANES), lambda b, p: (b, 0, p))
    n_br = len(DIL_CONFIGS)
    return pl.pallas_call(
        _dilated_kernel,
        grid=(B, HEAD_PAIRS),
        in_specs=[pl.BlockSpec(memory_space=pltpu.SMEM), blk, blk, blk],
        out_specs=blk,
        out_shape=jax.ShapeDtypeStruct((B, SEQ, DIL_WIDTH), BF16),
        scratch_shapes=[pltpu.VMEM((n_br, SEQ, LANES), F32),
                        pltpu.VMEM((n_br, SEQ, LANES), F32)],
        **_call_params("dilated", ("parallel", "parallel")),
    )(slopes, qa, ka, va)


def _fox_kernel(qt_ref, k_ref, c_ref, vt_ref, o_ref,
                s_ref, p_ref, m_ref, alpha_ref, acc_ref):
    first_head = pl.program_id(1) * FOX_HEADS_PER_STEP
    heads = range(FOX_HEADS_PER_STEP)
    tile = FOX_TILE
    kpos = lax.broadcasted_iota(jnp.int32, (tile, tile), 0)
    qpos = lax.broadcasted_iota(jnp.int32, (tile, tile), 1)
    causal = kpos <= qpos
    chan = lax.broadcasted_iota(jnp.int32, (LANES, tile), 0)

    def minus_c(head):
        hit = functools.reduce(jnp.logical_or,
                               [chan == head + N_FOX_HEADS * i for i in range(3)])
        return jnp.where(hit, -1.0, 0.0).astype(BF16)

    pick_c = [minus_c(first_head + h) for h in heads]
    ones_rows = jnp.ones((ONES_ROWS, tile), BF16)
    own = [chan < HEAD_DIM, chan >= HEAD_DIM]

    def tile_slice(t):
        return slice(t * tile, (t + 1) * tile)

    def query_columns(i):
        rhs = []
        for h in heads:
            qt = qt_ref[(h // 2) * LANES:(h // 2 + 1) * LANES, tile_slice(i)]
            rhs.append(jnp.concatenate(
                [jnp.where(own[h % 2], qt, jnp.zeros_like(qt)), pick_c[h]], axis=0))
        return rhs

    def store_scores(rhs, j, h):
        lhs = jnp.concatenate([k_ref[0, tile_slice(j), (h // 2) * LANES:(h // 2 + 1) * LANES],
                               c_ref[0, tile_slice(j), :]], axis=1)
        s_ref[h] = _dot(lhs, rhs[h])

    def accumulate(j, h):
        vt = vt_ref[h * HEAD_DIM:(h + 1) * HEAD_DIM, tile_slice(j)]
        vt_ones = jnp.concatenate([vt, ones_rows], axis=0)
        acc_ref[h] = alpha_ref[h] * acc_ref[h] + _dot(vt_ones, p_ref[h])

    def softmax_step(diagonal, h):
        s = jnp.where(causal, s_ref[h], NEG) if diagonal else s_ref[h]
        m = m_ref[h]
        m_new = jnp.maximum(m, jnp.max(s, axis=0, keepdims=True))
        m_ref[h] = m_new
        alpha_ref[h] = jnp.exp2(m - m_new)
        p_ref[h] = jnp.exp2(s - m_new).astype(BF16)

    def step(j_acc, diagonal, rhs, j_scores):
        for h in heads:
            if j_acc is not None:
                accumulate(j_acc, h)
            softmax_step(diagonal, h)
            if rhs is not None:
                store_scores(rhs, j_scores, h)

    def new_query_tile():
        for h in heads:
            m_ref[h] = jnp.full((1, tile), NEG, F32)

    def write_output(i):
        o_t = jnp.concatenate(
            [acc_ref[h, :HEAD_DIM, :] / acc_ref[h, HEAD_DIM:HEAD_DIM + 1, :] for h in heads],
            axis=0)
        o_ref[0, tile_slice(i), :] = o_t.T.astype(o_ref.dtype)

    n_tiles = SEQ // tile
    rhs0 = query_columns(0)
    for h in heads:
        acc_ref[h] = jnp.zeros((HEAD_DIM + ONES_ROWS, tile), F32)
        store_scores(rhs0, 0, h)
    new_query_tile()
    step(None, True, query_columns(1), 0)

    always = pl.program_id(0) >= 0
    for i in range(1, n_tiles):
        @pl.when(always)
        def _(i=i):
            rhs = query_columns(i)
            new_query_tile()
            step(i - 1, False, rhs, 1)
            write_output(i - 1)
            for j in range(1, i):
                step(j - 1, False, rhs, j + 1)
            step(i - 1, True, query_columns(i + 1) if i + 1 < n_tiles else None, 0)

    for h in heads:
        accumulate(n_tiles - 1, h)
    write_output(n_tiles - 1)


def _fox(qf_t, kf, c_pieces, vf_t):
    B = kf.shape[0]
    width = FOX_HEADS_PER_STEP * HEAD_DIM
    rows_blk = pl.BlockSpec((1, SEQ, width), lambda b, g: (b, 0, g))
    cols_blk = pl.BlockSpec((width, SEQ), lambda b, g: (g, b))
    return pl.pallas_call(
        _fox_kernel,
        grid=(B, N_FOX_HEADS // FOX_HEADS_PER_STEP),
        in_specs=[cols_blk, rows_blk, pl.BlockSpec((1, SEQ, LANES), lambda b, g: (b, 0, 0)),
                  cols_blk],
        out_specs=rows_blk,
        out_shape=jax.ShapeDtypeStruct((B, SEQ, FOX_WIDTH), BF16),
        scratch_shapes=[pltpu.VMEM((FOX_HEADS_PER_STEP, FOX_TILE, FOX_TILE), F32),
                        pltpu.VMEM((FOX_HEADS_PER_STEP, FOX_TILE, FOX_TILE), BF16),
                        pltpu.VMEM((FOX_HEADS_PER_STEP, 1, FOX_TILE), F32),
                        pltpu.VMEM((FOX_HEADS_PER_STEP, 1, FOX_TILE), F32),
                        pltpu.VMEM((FOX_HEADS_PER_STEP, HEAD_DIM + ONES_ROWS, FOX_TILE), F32)],
        **_call_params("fox", ("parallel", "parallel")),
    )(qf_t, kf, c_pieces, vf_t)


def _post_mix_kernel(x_ref, ya_ref, yf_ref, km_ref, vm_ref, woa_ref, wob_ref, g_ref,
                     wq_ref, wo_ref, o_ref):
    scale = LOG2E / math.sqrt(XATTN_HEAD_DIM)
    halves = [slice(0, ROW_TILE // 2), slice(ROW_TILE // 2, ROW_TILE)]
    xs = [x_ref[0, r, :] + _dot(ya_ref[0, r, :], woa_ref[...]) + _dot(yf_ref[0, r, :], wob_ref[...])
          for r in halves]
    qs = [(_dot(_rmsnorm(x, g_ref[...]).astype(BF16), wq_ref[...]) * scale).astype(BF16)
          for x in xs]

    def cols(hd):
        return slice(hd * XATTN_HEAD_DIM, (hd + 1) * XATTN_HEAD_DIM)

    for r, x, q in zip(halves, xs, qs):
        def scores(hd, q=q):
            return _dot_nt(q[:, cols(hd)], km_ref[0, :, cols(hd)])

        heads = []
        s = scores(0)
        for hd in range(N_XATTN_HEADS):
            s_next = scores(hd + 1) if hd + 1 < N_XATTN_HEADS else None
            p = jnp.exp2(s - jnp.max(s, axis=-1, keepdims=True))
            l = jnp.sum(p, axis=-1, keepdims=True)
            heads.append((_dot(p.astype(BF16), vm_ref[0, :, cols(hd)]) / l).astype(BF16))
            s = s_next
        o_ref[0, r, :] = x + _dot(jnp.concatenate(heads, axis=-1), wo_ref[...])


def _post_mix(x, ya, yf, km, vm, w_oa, w_ob, g_xattn, w_xq, w_xo):
    B = x.shape[0]
    tok = lambda b, i: (b, i, 0)
    per_b = lambda b, i: (b, 0, 0)
    return pl.pallas_call(
        _post_mix_kernel,
        grid=(B, SEQ // ROW_TILE),
        in_specs=[pl.BlockSpec((1, ROW_TILE, D_MODEL), tok),
                  pl.BlockSpec((1, ROW_TILE, DIL_WIDTH), tok),
                  pl.BlockSpec((1, ROW_TILE, FOX_WIDTH), tok),
                  pl.BlockSpec((1, N_MEM, D_MODEL), per_b),
                  pl.BlockSpec((1, N_MEM, D_MODEL), per_b),
                  _resident((DIL_WIDTH, D_MODEL)), _resident((FOX_WIDTH, D_MODEL)),
                  _resident((1, D_MODEL)), _resident((D_MODEL, D_MODEL)),
                  _resident((D_MODEL, D_MODEL))],
        out_specs=pl.BlockSpec((1, ROW_TILE, D_MODEL), tok),
        out_shape=jax.ShapeDtypeStruct(x.shape, F32),
        **_call_params("post_mix", ("parallel", "parallel")),
    )(x, ya, yf, km, vm, w_oa, w_ob, g_xattn, w_xq, w_xo)


def _mlp_kernel(x_ref, g_ref, wu_ref, wd_ref, gf_ref, o_ref):
    x = x_ref[...]
    h = _rmsnorm(x, g_ref[...]).astype(BF16)
    y = x
    for c in range(D_FF // FF_CHUNK):
        cols = slice(c * FF_CHUNK, (c + 1) * FF_CHUNK)
        a = jnp.maximum(_dot(h, wu_ref[:, cols]), 0.0)
        y = y + _dot((a * a).astype(BF16), wd_ref[cols, :])
    o_ref[...] = _rmsnorm(y, gf_ref[...])


def _mlp(x2d, g_mlp, w_up, w_down, g_final):
    T = x2d.shape[0]
    row = pl.BlockSpec((ROW_TILE, D_MODEL), lambda i: (i, 0))
    return pl.pallas_call(
        _mlp_kernel,
        grid=(T // ROW_TILE,),
        in_specs=[row, _resident((1, D_MODEL)), _resident((D_MODEL, D_FF)),
                  _resident((D_FF, D_MODEL)), _resident((1, D_MODEL))],
        out_specs=row,
        out_shape=jax.ShapeDtypeStruct(x2d.shape, F32),
        **_call_params("mlp", ("parallel",)),
    )(x2d, g_mlp, w_up, w_down, g_final)


def kernel(x, mem, g_mix, w_in, b_forget, w_out, g_xattn, g_mem, w_xq, w_xk, w_xv, w_xo,
           g_mlp, w_up, w_down, g_final):
    B, S, D = x.shape
    assert (S, D) == (SEQ, D_MODEL) and mem.shape == (B, N_MEM, D_MODEL)
    gain = lambda g: g.reshape(1, D_MODEL).astype(F32)
    bf = lambda w: w.astype(BF16)
    slopes = 2.0 ** (-(jnp.arange(1, N_DIL_HEADS + 1, dtype=F32) * (8.0 / N_DIL_HEADS)))

    km, vm = _mem_kv(mem, gain(g_mem), bf(w_xk), bf(w_xv))
    o_qf = 3 * DIL_WIDTH
    o_kf, o_vf = o_qf + FOX_WIDTH, o_qf + 2 * FOX_WIDTH
    pad_heads = lambda a: jnp.pad(a, ((0, 0), (0, LANES - N_FOX_HEADS)))
    qa, ka, va, qf_t, kf, vf_t, gate = _in_proj(
        x.reshape(B * S, D), gain(g_mix), bf(w_in[:, :o_qf]), bf(w_in[:, o_kf:o_vf]),
        bf(w_in[:, o_qf:o_kf].T), bf(w_in[:, o_vf:QKV_WIDTH].T), bf(pad_heads(w_in[:, QKV_WIDTH:])))
    c_pieces = _gate_cum(gate, pad_heads(b_forget.reshape(1, N_FOX_HEADS).astype(F32)), B)
    seq = lambda a: a.reshape(B, S, a.shape[-1])
    ya = _dilated(slopes, seq(qa), seq(ka), seq(va))
    yf = _fox(qf_t, seq(kf), c_pieces, vf_t)
    x = _post_mix(x, ya, yf, km, vm, bf(w_out[:DIL_WIDTH]), bf(w_out[DIL_WIDTH:]),
                  gain(g_xattn), bf(w_xq), bf(w_xo))
    y = _mlp(x.reshape(B * S, D), gain(g_mlp), bf(w_up), bf(w_down), gain(g_final))
    return y.reshape(B, S, D)
```

```python
import functools
import math

import jax
import jax.numpy as jnp
from jax import lax
from jax.experimental import pallas as pl
from jax.experimental.pallas import tpu as pltpu

D_MODEL = 1024
SEQ = 2048
N_MEM = 256
HEAD_DIM = 64
N_DIL_HEADS = 8
N_FOX_HEADS = 8
DIL_WIDTH = N_DIL_HEADS * HEAD_DIM
FOX_WIDTH = N_FOX_HEADS * HEAD_DIM
QKV_WIDTH = 3 * DIL_WIDTH + 3 * FOX_WIDTH
DIL_CONFIGS = ((128, 1), (512, 4), (2048, 16))
BLOCK = 128
N_XATTN_HEADS = 4
XATTN_HEAD_DIM = D_MODEL // N_XATTN_HEADS
D_FF = 4 * D_MODEL
EPS = 1e-6
NEG = -1e30
LOG2E = math.log2(math.e)

LANES = 128
HEAD_PAIRS = N_FOX_HEADS // 2
ROW_TILE = 1024
FOX_TILE = 256
FOX_HEADS_PER_STEP = 8
ONES_ROWS = 16
FF_CHUNK = 1024
MIX_CHUNK = 256

F32 = jnp.float32
BF16 = jnp.bfloat16

_NT = (((1,), (1,)), ((), ()))


def _dot(a, b):
    return jnp.dot(a, b, preferred_element_type=F32)


def _dot_nt(a, b):
    return lax.dot_general(a, b, _NT, preferred_element_type=F32)


def _rmsnorm(x, g):
    return x * lax.rsqrt(jnp.mean(x * x, axis=-1, keepdims=True) + EPS) * g


def _resident(shape):
    zeros = (0,) * len(shape)
    return pl.BlockSpec(shape, lambda *_: zeros, pipeline_mode=pl.Buffered(1))


VMEM_MIB = {"mem_kv": 24, "in_proj": 52, "gate_cum": 16, "dilated": 32, "fox": 32,
            "post_mix": 56, "mlp": 58}


def _call_params(name, semantics):
    return dict(name=name, compiler_params=pltpu.CompilerParams(
        dimension_semantics=semantics, vmem_limit_bytes=VMEM_MIB[name] * 1024 * 1024))


def _stack_heads(q2):
    lane = lax.broadcasted_iota(jnp.int32, (1, LANES), 1)
    zero = jnp.zeros_like(q2)
    return jnp.concatenate([jnp.where(lane < HEAD_DIM, q2, zero),
                            jnp.where(lane >= HEAD_DIM, q2, zero)], axis=0)


def _unstack_heads(o):
    rows = o.shape[0] // 2
    lane = lax.broadcasted_iota(jnp.int32, (1, LANES), 1)
    return jnp.where(lane < HEAD_DIM, o[:rows], o[rows:])


def _mem_kv_kernel(mem_ref, g_ref, wk_ref, wv_ref, k_ref, v_ref):
    m = _rmsnorm(mem_ref[0], g_ref[...]).astype(BF16)
    k_ref[0] = _dot(m, wk_ref[...]).astype(BF16)
    v_ref[0] = _dot(m, wv_ref[...]).astype(BF16)


def _mem_kv(mem, g_mem, w_xk, w_xv):
    B = mem.shape[0]
    blk = pl.BlockSpec((1, N_MEM, D_MODEL), lambda b: (b, 0, 0))
    return pl.pallas_call(
        _mem_kv_kernel,
        grid=(B,),
        in_specs=[blk, _resident((1, D_MODEL)), _resident((D_MODEL, D_MODEL)),
                  _resident((D_MODEL, D_MODEL))],
        out_specs=[blk, blk],
        out_shape=[jax.ShapeDtypeStruct((B, N_MEM, D_MODEL), BF16)] * 2,
        **_call_params("mem_kv", ("parallel",)),
    )(mem, g_mem, w_xk, w_xv)


def _in_proj_kernel(x_ref, g_ref, wa_ref, wk_ref, wqt_ref, wvt_ref, wg_ref,
                    qa, ka, va, qft, kf, vft, gate):
    scale = LOG2E / math.sqrt(HEAD_DIM)
    for r in (slice(0, ROW_TILE // 2), slice(ROW_TILE // 2, ROW_TILE)):
        h = _rmsnorm(x_ref[r, :], g_ref[...]).astype(BF16)
        qa[r, :] = _dot(h, wa_ref[:, :DIL_WIDTH]) * scale
        ka[r, :] = _dot(h, wa_ref[:, DIL_WIDTH:2 * DIL_WIDTH])
        va[r, :] = _dot(h, wa_ref[:, 2 * DIL_WIDTH:])
        qft[:, r] = (_dot_nt(wqt_ref[...], h) * scale).astype(BF16)
        kf[r, :] = _dot(h, wk_ref[...]).astype(BF16)
        vft[:, r] = _dot_nt(wvt_ref[...], h).astype(BF16)
        gate[r, :] = _dot(h, wg_ref[...])


def _in_proj(x2d, g_mix, w_a, w_kf, w_qf_t, w_vf_t, w_gate):
    T = x2d.shape[0]
    row = lambda i: (i, 0)
    col = lambda i: (0, i)
    rows_blk = pl.BlockSpec((ROW_TILE, DIL_WIDTH), row)
    cols_blk = pl.BlockSpec((FOX_WIDTH, ROW_TILE), col)
    return pl.pallas_call(
        _in_proj_kernel,
        grid=(T // ROW_TILE,),
        in_specs=[pl.BlockSpec((ROW_TILE, D_MODEL), row), _resident((1, D_MODEL)),
                  _resident((D_MODEL, 3 * DIL_WIDTH)), _resident((D_MODEL, FOX_WIDTH)),
                  _resident((FOX_WIDTH, D_MODEL)), _resident((FOX_WIDTH, D_MODEL)),
                  _resident((D_MODEL, LANES))],
        out_specs=[rows_blk] * 3 + [cols_blk, rows_blk, cols_blk,
                                    pl.BlockSpec((ROW_TILE, LANES), row)],
        out_shape=[jax.ShapeDtypeStruct((T, DIL_WIDTH), F32)] * 3
        + [jax.ShapeDtypeStruct((FOX_WIDTH, T), BF16), jax.ShapeDtypeStruct((T, FOX_WIDTH), BF16),
           jax.ShapeDtypeStruct((FOX_WIDTH, T), BF16), jax.ShapeDtypeStruct((T, LANES), F32)],
        **_call_params("in_proj", ("parallel",)),
    )(x2d, g_mix, w_a, w_kf, w_qf_t, w_vf_t, w_gate)


def _split3(x):
    hi = x.astype(BF16)
    rest = x - hi.astype(F32)
    mid = rest.astype(BF16)
    lo = (rest - mid.astype(F32)).astype(BF16)
    return hi, mid, lo


def _gate_cum_kernel(gate_ref, b_ref, c_ref):
    r = lax.broadcasted_iota(jnp.int32, (BLOCK, BLOCK), 0)
    c = lax.broadcasted_iota(jnp.int32, (BLOCK, BLOCK), 1)
    lower = (r >= c).astype(BF16)
    is_head = lax.broadcasted_iota(jnp.int32, (1, LANES), 1) < N_FOX_HEADS
    carry = jnp.zeros((1, LANES), F32)
    for blk in range(SEQ // BLOCK):
        rows = slice(blk * BLOCK, (blk + 1) * BLOCK)
        z = gate_ref[rows, :] + b_ref[...]
        log_f = jnp.minimum(z, 0.0) - jnp.log1p(jnp.exp(-jnp.abs(z)))
        cum = sum(_dot(lower, piece) for piece in _split3(log_f)) + carry
        carry = cum[BLOCK - 1:BLOCK, :]
        packed = sum(pltpu.roll(jnp.where(is_head, piece.astype(F32), 0.0), N_FOX_HEADS * i, axis=1)
                     if i else jnp.where(is_head, piece.astype(F32), 0.0)
                     for i, piece in enumerate(_split3(cum * LOG2E)))
        c_ref[0, rows, :] = packed.astype(BF16)


def _gate_cum(gate, b_forget, B):
    return pl.pallas_call(
        _gate_cum_kernel,
        grid=(B,),
        in_specs=[pl.BlockSpec((SEQ, LANES), lambda b: (b, 0)), _resident((1, LANES))],
        out_specs=pl.BlockSpec((1, SEQ, LANES), lambda b: (b, 0, 0)),
        out_shape=jax.ShapeDtypeStruct((B, SEQ, LANES), BF16),
        **_call_params("gate_cum", ("parallel",)),
    )(gate, b_forget)


def _dilated_kernel(slopes_ref, q_ref, k_ref, v_ref, o_ref, ob_ref, lb_ref):
    pair = pl.program_id(1)
    row = lax.broadcasted_iota(jnp.int32, (2 * BLOCK, 1), 0)
    slope = jnp.where(row < BLOCK, slopes_ref[2 * pair], slopes_ref[2 * pair + 1])
    qi = jnp.where(row < BLOCK, row, row - BLOCK)

    def rows(start, size, dilation):
        if dilation == 1:
            return pl.ds(start, size)
        return pl.ds(start, size, stride=dilation)

    for branch, (window, dilation) in enumerate(DIL_CONFIGS):
        steps = window // dilation
        length = SEQ // dilation
        n_blocks = length // BLOCK

        def bias_table(n_keys, first_key):
            kj = lax.broadcasted_iota(jnp.int32, (1, n_keys), 1) + first_key
            delta = qi - kj
            valid = (delta >= 0) & (delta <= steps)
            dist = (delta * dilation).astype(F32)
            return jnp.where(valid, (-LOG2E) * slope * dist, NEG)

        bias_first = bias_table(BLOCK, 0)
        bias_rest = bias_table(2 * BLOCK, -BLOCK)

        def attend(q_start, k_start, n_keys, bias):
            q2 = q_ref[0, rows(q_start, BLOCK, dilation), :].astype(BF16)
            k2 = k_ref[0, rows(k_start, n_keys, dilation), :].astype(BF16)
            v2 = v_ref[0, rows(k_start, n_keys, dilation), :].astype(BF16)
            s = _dot_nt(_stack_heads(q2), k2) + bias
            m = jnp.max(s, axis=-1, keepdims=True)
            p = jnp.exp2(s - m)
            l = jnp.sum(p, axis=-1, keepdims=True)
            o = _dot(p.astype(BF16), v2) / l
            lse = jnp.broadcast_to(m + jnp.log(l) * LOG2E, (2 * BLOCK, LANES))
            ob_ref[branch, rows(q_start, BLOCK, dilation), :] = _unstack_heads(o)
            lb_ref[branch, rows(q_start, BLOCK, dilation), :] = _unstack_heads(lse)

        for r in range(dilation):
            for n in range(n_blocks):
                q_start = r + n * dilation * BLOCK
                if n == 0:
                    attend(q_start, q_start, BLOCK, bias_first)
                else:
                    attend(q_start, q_start - dilation * BLOCK, 2 * BLOCK, bias_rest)

    def mix(i, carry):
        chunk = pl.ds(pl.multiple_of(i * MIX_CHUNK, MIX_CHUNK), MIX_CHUNK)
        lses = [lb_ref[b, chunk, :] for b in range(len(DIL_CONFIGS))]
        top = functools.reduce(jnp.maximum, lses)
        ws = [jnp.exp2(l - top) for l in lses]
        num = sum(w * ob_ref[b, chunk, :] for b, w in enumerate(ws))
        o_ref[0, chunk, :] = (num / sum(ws)).astype(o_ref.dtype)
        return carry

    lax.fori_loop(0, SEQ // MIX_CHUNK, mix, 0)


def _dilated(slopes, qa, ka, va):
    B = qa.shape[0]
    blk = pl.BlockSpec((1, SEQ, LANES), lambda b, p: (b, 0, p))
    n_br = len(DIL_CONFIGS)
    return pl.pallas_call(
        _dilated_kernel,
        grid=(B, HEAD_PAIRS),
        in_specs=[pl.BlockSpec(memory_space=pltpu.SMEM), blk, blk, blk],
        out_specs=blk,
        out_shape=jax.ShapeDtypeStruct((B, SEQ, DIL_WIDTH), BF16),
        scratch_shapes=[pltpu.VMEM((n_br, SEQ, LANES), F32),
                        pltpu.VMEM((n_br, SEQ, LANES), F32)],
        **_call_params("dilated", ("parallel", "parallel")),
    )(slopes, qa, ka, va)


def _fox_kernel(qt_ref, k_ref, c_ref, vt_ref, o_ref,
                s_ref, p_ref, m_ref, alpha_ref, acc_ref):
    first_head = pl.program_id(1) * FOX_HEADS_PER_STEP
    heads = range(FOX_HEADS_PER_STEP)
    tile = FOX_TILE
    kpos = lax.broadcasted_iota(jnp.int32, (tile, tile), 0)
    qpos = lax.broadcasted_iota(jnp.int32, (tile, tile), 1)
    causal = kpos <= qpos
    chan = lax.broadcasted_iota(jnp.int32, (LANES, tile), 0)

    def minus_c(head):
        hit = functools.reduce(jnp.logical_or,
                               [chan == head + N_FOX_HEADS * i for i in range(3)])
        return jnp.where(hit, -1.0, 0.0).astype(BF16)

    pick_c = [minus_c(first_head + h) for h in heads]
    ones_rows = jnp.ones((ONES_ROWS, tile), BF16)
    own = [chan < HEAD_DIM, chan >= HEAD_DIM]

    def tile_slice(t):
        return slice(t * tile, (t + 1) * tile)

    def query_columns(i):
        rhs = []
        for h in heads:
            qt = qt_ref[(h // 2) * LANES:(h // 2 + 1) * LANES, tile_slice(i)]
            rhs.append(jnp.concatenate(
                [jnp.where(own[h % 2], qt, jnp.zeros_like(qt)), pick_c[h]], axis=0))
        return rhs

    def store_scores(rhs, j, h):
        lhs = jnp.concatenate([k_ref[0, tile_slice(j), (h // 2) * LANES:(h // 2 + 1) * LANES],
                               c_ref[0, tile_slice(j), :]], axis=1)
        s_ref[h] = _dot(lhs, rhs[h])

    def accumulate(j, h):
        vt = vt_ref[h * HEAD_DIM:(h + 1) * HEAD_DIM, tile_slice(j)]
        vt_ones = jnp.concatenate([vt, ones_rows], axis=0)
        acc_ref[h] = alpha_ref[h] * acc_ref[h] + _dot(vt_ones, p_ref[h])

    def softmax_step(diagonal, h):
        s = jnp.where(causal, s_ref[h], NEG) if diagonal else s_ref[h]
        m = m_ref[h]
        m_new = jnp.maximum(m, jnp.max(s, axis=0, keepdims=True))
        m_ref[h] = m_new
        alpha_ref[h] = jnp.exp2(m - m_new)
        p_ref[h] = jnp.exp2(s - m_new).astype(BF16)

    def step(j_acc, diagonal, rhs, j_scores):
        for h in heads:
            if j_acc is not None:
                accumulate(j_acc, h)
            softmax_step(diagonal, h)
            if rhs is not None:
                store_scores(rhs, j_scores, h)

    def new_query_tile():
        for h in heads:
            m_ref[h] = jnp.full((1, tile), NEG, F32)

    def write_output(i):
        o_t = jnp.concatenate(
            [acc_ref[h, :HEAD_DIM, :] / acc_ref[h, HEAD_DIM:HEAD_DIM + 1, :] for h in heads],
            axis=0)
        o_ref[0, tile_slice(i), :] = o_t.T.astype(o_ref.dtype)

    n_tiles = SEQ // tile
    rhs0 = query_columns(0)
    for h in heads:
        acc_ref[h] = jnp.zeros((HEAD_DIM + ONES_ROWS, tile), F32)
        store_scores(rhs0, 0, h)
    new_query_tile()
    step(None, True, query_columns(1), 0)

    always = pl.program_id(0) >= 0
    for i in range(1, n_tiles):
        @pl.when(always)
        def _(i=i):
            rhs = query_columns(i)
            new_query_tile()
            step(i - 1, False, rhs, 1)
            write_output(i - 1)
            for j in range(1, i):
                step(j - 1, False, rhs, j + 1)
            step(i - 1, True, query_columns(i + 1) if i + 1 < n_tiles else None, 0)

    for h in heads:
        accumulate(n_tiles - 1, h)
    write_output(n_tiles - 1)


def _fox(qf_t, kf, c_pieces, vf_t):
    B = kf.shape[0]
    width = FOX_HEADS_PER_STEP * HEAD_DIM
    rows_blk = pl.BlockSpec((1, SEQ, width), lambda b, g: (b, 0, g))
    cols_blk = pl.BlockSpec((width, SEQ), lambda b, g: (g, b))
    return pl.pallas_call(
        _fox_kernel,
        grid=(B, N_FOX_HEADS // FOX_HEADS_PER_STEP),
        in_specs=[cols_blk, rows_blk, pl.BlockSpec((1, SEQ, LANES), lambda b, g: (b, 0, 0)),
                  cols_blk],
        out_specs=rows_blk,
        out_shape=jax.ShapeDtypeStruct((B, SEQ, FOX_WIDTH), BF16),
        scratch_shapes=[pltpu.VMEM((FOX_HEADS_PER_STEP, FOX_TILE, FOX_TILE), F32),
                        pltpu.VMEM((FOX_HEADS_PER_STEP, FOX_TILE, FOX_TILE), BF16),
                        pltpu.VMEM((FOX_HEADS_PER_STEP, 1, FOX_TILE), F32),
                        pltpu.VMEM((FOX_HEADS_PER_STEP, 1, FOX_TILE), F32),
                        pltpu.VMEM((FOX_HEADS_PER_STEP, HEAD_DIM + ONES_ROWS, FOX_TILE), F32)],
        **_call_params("fox", ("parallel", "parallel")),
    )(qf_t, kf, c_pieces, vf_t)


def _post_mix_kernel(x_ref, ya_ref, yf_ref, km_ref, vm_ref, woa_ref, wob_ref, g_ref,
                     wq_ref, wo_ref, o_ref):
    scale = LOG2E / math.sqrt(XATTN_HEAD_DIM)
    halves = [slice(0, ROW_TILE // 2), slice(ROW_TILE // 2, ROW_TILE)]
    xs = [x_ref[0, r, :] + _dot(ya_ref[0, r, :], woa_ref[...]) + _dot(yf_ref[0, r, :], wob_ref[...])
          for r in halves]
    qs = [(_dot(_rmsnorm(x, g_ref[...]).astype(BF16), wq_ref[...]) * scale).astype(BF16)
          for x in xs]

    def cols(hd):
        return slice(hd * XATTN_HEAD_DIM, (hd + 1) * XATTN_HEAD_DIM)

    for r, x, q in zip(halves, xs, qs):
        def scores(hd, q=q):
            return _dot_nt(q[:, cols(hd)], km_ref[0, :, cols(hd)])

        heads = []
        s = scores(0)
        for hd in range(N_XATTN_HEADS):
            s_next = scores(hd + 1) if hd + 1 < N_XATTN_HEADS else None
            p = jnp.exp2(s - jnp.max(s, axis=-1, keepdims=True))
            l = jnp.sum(p, axis=-1, keepdims=True)
            heads.append((_dot(p.astype(BF16), vm_ref[0, :, cols(hd)]) / l).astype(BF16))
            s = s_next
        o_ref[0, r, :] = x + _dot(jnp.concatenate(heads, axis=-1), wo_ref[...])


def _post_mix(x, ya, yf, km, vm, w_oa, w_ob, g_xattn, w_xq, w_xo):
    B = x.shape[0]
    tok = lambda b, i: (b, i, 0)
    per_b = lambda b, i: (b, 0, 0)
    return pl.pallas_call(
        _post_mix_kernel,
        grid=(B, SEQ // ROW_TILE),
        in_specs=[pl.BlockSpec((1, ROW_TILE, D_MODEL), tok),
                  pl.BlockSpec((1, ROW_TILE, DIL_WIDTH), tok),
                  pl.BlockSpec((1, ROW_TILE, FOX_WIDTH), tok),
                  pl.BlockSpec((1, N_MEM, D_MODEL), per_b),
                  pl.BlockSpec((1, N_MEM, D_MODEL), per_b),
                  _resident((DIL_WIDTH, D_MODEL)), _resident((FOX_WIDTH, D_MODEL)),
                  _resident((1, D_MODEL)), _resident((D_MODEL, D_MODEL)),
                  _resident((D_MODEL, D_MODEL))],
        out_specs=pl.BlockSpec((1, ROW_TILE, D_MODEL), tok),
        out_shape=jax.ShapeDtypeStruct(x.shape, F32),
        **_call_params("post_mix", ("parallel", "parallel")),
    )(x, ya, yf, km, vm, w_oa, w_ob, g_xattn, w_xq, w_xo)


def _mlp_kernel(x_ref, g_ref, wu_ref, wd_ref, gf_ref, o_ref):
    x = x_ref[...]
    h = _rmsnorm(x, g_ref[...]).astype(BF16)
    y = x
    for c in range(D_FF // FF_CHUNK):
        cols = slice(c * FF_CHUNK, (c + 1) * FF_CHUNK)
        a = jnp.maximum(_dot(h, wu_ref[:, cols]), 0.0)
        y = y + _dot((a * a).astype(BF16), wd_ref[cols, :])
    o_ref[...] = _rmsnorm(y, gf_ref[...])


def _mlp(x2d, g_mlp, w_up, w_down, g_final):
    T = x2d.shape[0]
    row = pl.BlockSpec((ROW_TILE, D_MODEL), lambda i: (i, 0))
    return pl.pallas_call(
        _mlp_kernel,
        grid=(T // ROW_TILE,),
        in_specs=[row, _resident((1, D_MODEL)), _resident((D_MODEL, D_FF)),
                  _resident((D_FF, D_MODEL)), _resident((1, D_MODEL))],
        out_specs=row,
        out_shape=jax.ShapeDtypeStruct(x2d.shape, F32),
        **_call_params("mlp", ("parallel",)),
    )(x2d, g_mlp, w_up, w_down, g_final)


def kernel(x, mem, g_mix, w_in, b_forget, w_out, g_xattn, g_mem, w_xq, w_xk, w_xv, w_xo,
           g_mlp, w_up, w_down, g_final):
    B, S, D = x.shape
    assert (S, D) == (SEQ, D_MODEL) and mem.shape == (B, N_MEM, D_MODEL)
    gain = lambda g: g.reshape(1, D_MODEL).astype(F32)
    bf = lambda w: w.astype(BF16)
    slopes = 2.0 ** (-(jnp.arange(1, N_DIL_HEADS + 1, dtype=F32) * (8.0 / N_DIL_HEADS)))

    km, vm = _mem_kv(mem, gain(g_mem), bf(w_xk), bf(w_xv))
    o_qf = 3 * DIL_WIDTH
    o_kf, o_vf = o_qf + FOX_WIDTH, o_qf + 2 * FOX_WIDTH
    pad_heads = lambda a: jnp.pad(a, ((0, 0), (0, LANES - N_FOX_HEADS)))
    qa, ka, va, qf_t, kf, vf_t, gate = _in_proj(
        x.reshape(B * S, D), gain(g_mix), bf(w_in[:, :o_qf]), bf(w_in[:, o_kf:o_vf]),
        bf(w_in[:, o_qf:o_kf].T), bf(w_in[:, o_vf:QKV_WIDTH].T), bf(pad_heads(w_in[:, QKV_WIDTH:])))
    c_pieces = _gate_cum(gate, pad_heads(b_forget.reshape(1, N_FOX_HEADS).astype(F32)), B)
    seq = lambda a: a.reshape(B, S, a.shape[-1])
    ya = _dilated(slopes, seq(qa), seq(ka), seq(va))
    yf = _fox(qf_t, seq(kf), c_pieces, vf_t)
    x = _post_mix(x, ya, yf, km, vm, bf(w_out[:DIL_WIDTH]), bf(w_out[DIL_WIDTH:]),
                  gain(g_xattn), bf(w_xq), bf(w_xo))
    y = _mlp(x.reshape(B * S, D), gain(g_mlp), bf(w_up), bf(w_down), gain(g_final))
    return y.reshape(B, S, D)
```
